```python
import math
import jax, jax.numpy as jnp
from jax import lax
import numpy as np

D_MODEL = 2048
BATCH = 2
SEQ = 4096
DEPTH = 4
DEC_BATCH = 8
DEC_SEQ = 8
PAST_LEN = 16384
PAGE_SIZE = 128

N_HEADS = 16
HEAD_DIM = D_MODEL // N_HEADS
N_KV_HEADS = N_HEADS
KV_DIM = N_KV_HEADS * HEAD_DIM
MOBA_BLOCK = 256
MOBA_TOP_K = 3
Q_BLOCK = 32
CHUNK = 128
D_GMLP = 2 * D_MODEL
N_SGU_GROUPS = 16
SGU_GROUP = D_GMLP // N_SGU_GROUPS
D_FF = 4 * D_MODEL
N_A_LAYERS = DEPTH // 2
N_B_LAYERS = DEPTH - N_A_LAYERS
NORM_EPS = 1e-6

kernel_name = 'yoco_gmlp_moba_decoder_step'


def _rmsnorm(x, g):
    x32 = x.astype(jnp.float32)
    y = x32 * lax.rsqrt(jnp.mean(x32 * x32, axis=-1, keepdims=True) + NORM_EPS)
    return y.astype(x.dtype) * g


def _layernorm(x, g, b):
    x32 = x.astype(jnp.float32)
    mu = jnp.mean(x32, axis=-1, keepdims=True)
    xc = x32 - mu
    var = jnp.mean(xc * xc, axis=-1, keepdims=True)
    return (xc * lax.rsqrt(var + NORM_EPS)).astype(x.dtype) * g + b


def _alibi_slopes():
    return jnp.exp2(-8.0 * jnp.arange(1, N_HEADS + 1, dtype=jnp.float32) / N_HEADS)


def _channel_mlp(x, g, w1, w2):
    h = jax.nn.relu(_rmsnorm(x, g) @ w1)
    return x + (h * h) @ w2


def _gmlp_layer(x, g, w_in, ln_g, ln_b, w_s, b_s, w_out):
    bsz, s, _ = x.shape
    L = min(s, CHUNK)
    n = s // L
    z = jax.nn.gelu(_rmsnorm(x, g) @ w_in)
    u, v = jnp.split(z, 2, axis=-1)
    v = _layernorm(v, ln_g, ln_b)
    ws = jnp.tril(w_s[:, :L, :L])
    vg = v.reshape(bsz, n, L, N_SGU_GROUPS, SGU_GROUP)
    mixed = jnp.einsum('gts,bnsgc->bntgc', ws, vg) + b_s[:, :L].T[:, :, None]
    h = u * mixed.reshape(bsz, s, D_GMLP)
    return x + h @ w_out, v


def _shared_kv(x, g, w_k, w_v):
    bsz, s, _ = x.shape
    h = _rmsnorm(x, g)
    k = (h @ w_k).reshape(bsz, s, N_KV_HEADS, HEAD_DIM)
    v = (h @ w_v).reshape(bsz, s, N_KV_HEADS, HEAD_DIM)
    return k, v


def _prompt_blocks(k, v):
    bsz, s = k.shape[:2]
    nb = -(-s // MOBA_BLOCK)
    pad = nb * MOBA_BLOCK - s

    def blk(a):
        a = jnp.pad(a, ((0, 0), (0, pad), (0, 0), (0, 0)))
        return a.reshape(bsz, nb, MOBA_BLOCK, N_KV_HEADS, HEAD_DIM).transpose(0, 3, 1, 2, 4)

    kb, vb = blk(k), blk(v)
    kmean = jnp.mean(kb.astype(jnp.float32), axis=3)
    return kb, vb, kmean


def _moba_prompt(q, kb, vb, kmean, slopes):
    bsz, s = q.shape[:2]
    nb = kb.shape[2]
    k_sel = min(MOBA_TOP_K, nb)
    n_qb = s // Q_BLOCK
    scale = HEAD_DIM ** -0.5
    qs = q.reshape(bsz, n_qb, Q_BLOCK, N_HEADS, HEAD_DIM).transpose(1, 0, 3, 2, 4)
    bi = jnp.arange(bsz)[:, None, None, None]
    hi = jnp.arange(N_HEADS)[None, :, None, None]
    offs = jnp.arange(MOBA_BLOCK)
    m4 = slopes[None, :, None, None]
    m5 = slopes[None, :, None, None, None]

    def one(args):
        qb, i = args
        pos_q = i * Q_BLOCK + jnp.arange(Q_BLOCK)
        ob = (i * Q_BLOCK) // MOBA_BLOCK
        gate = jnp.einsum('bhqd,bhnd->bhqn', qb.astype(jnp.float32), kmean)
        gate = jnp.where(jnp.arange(nb) < ob, gate, -jnp.inf)
        _, idx = lax.top_k(gate, k_sel)
        valid = jnp.arange(k_sel) < ob
        ks = kb[bi, hi, idx]
        vs = vb[bi, hi, idx]
        s_sel = jnp.einsum('bhqd,bhqrkd->bhqrk', qb, ks, preferred_element_type=jnp.float32) * scale
        dist = (pos_q[:, None, None] - (idx[..., None] * MOBA_BLOCK + offs)).astype(jnp.float32)
        s_sel = jnp.where(valid[:, None], s_sel - m5 * jnp.abs(dist), -jnp.inf)
        k_own = lax.dynamic_index_in_dim(kb, ob, axis=2, keepdims=False)
        v_own = lax.dynamic_index_in_dim(vb, ob, axis=2, keepdims=False)
        dist_own = pos_q[:, None] - (ob * MOBA_BLOCK + offs)[None, :]
        s_own = jnp.einsum('bhqd,bhkd->bhqk', qb, k_own, preferred_element_type=jnp.float32) * scale
        s_own = jnp.where(dist_own >= 0, s_own - m4 * dist_own.astype(jnp.float32), -jnp.inf)
        n_sel = k_sel * MOBA_BLOCK
        scores = jnp.concatenate([s_sel.reshape(bsz, N_HEADS, Q_BLOCK, n_sel), s_own], axis=-1)
        p = jax.nn.softmax(scores, axis=-1).astype(qb.dtype)
        p_sel = p[..., :n_sel].reshape(bsz, N_HEADS, Q_BLOCK, k_sel, MOBA_BLOCK)
        p_own = p[..., n_sel:]
        return (jnp.einsum('bhqrk,bhqrkd->bhqd', p_sel, vs)
                + jnp.einsum('bhqk,bhkd->bhqd', p_own, v_own))

    outs = lax.map(one, (qs, jnp.arange(n_qb)))
    return outs.transpose(1, 0, 3, 2, 4).reshape(bsz, s, N_HEADS * HEAD_DIM)


def _cached_block_means(cache_k, page_table):
    db, n_pages = page_table.shape
    page = cache_k.shape[1]
    nfb = (n_pages * page) // MOBA_BLOCK
    ppb = MOBA_BLOCK // page
    psum = jnp.sum(cache_k, axis=1, dtype=jnp.float32)
    bsum = psum[page_table[:, :nfb * ppb]].reshape(db, nfb, ppb, N_KV_HEADS, HEAD_DIM).sum(axis=2)
    return (bsum / MOBA_BLOCK).transpose(0, 2, 1, 3)


def _moba_sample(q, k_new, v_new, cache_k, cache_v, page_table, kmean, slopes):
    db, t = q.shape[:2]
    page = cache_k.shape[1]
    n_pages = page_table.shape[1]
    past_len = n_pages * page
    ppb = MOBA_BLOCK // page
    nfb = past_len // MOBA_BLOCK
    n_own_past = past_len - nfb * MOBA_BLOCK
    scale = HEAD_DIM ** -0.5
    qh = q.transpose(0, 2, 1, 3)
    pos_q = past_len + jnp.arange(t)
    m4 = slopes[None, :, None, None]
    s_list, v_list, per_query = [], [], []
    if nfb > 0:
        k_sel = min(MOBA_TOP_K, nfb)
        gate = jnp.einsum('bhtd,bhnd->bhtn', qh.astype(jnp.float32), kmean)
        _, idx = lax.top_k(gate, k_sel)
        logical = idx[..., None] * ppb + jnp.arange(ppb)
        phys = page_table[jnp.arange(db)[:, None, None, None, None], logical]
        hv = jnp.arange(N_KV_HEADS)[None, :, None, None, None]
        n_sel = k_sel * MOBA_BLOCK
        ks = cache_k[phys, :, hv].reshape(db, N_HEADS, t, n_sel, HEAD_DIM)
        vs = cache_v[phys, :, hv].reshape(db, N_HEADS, t, n_sel, HEAD_DIM)
        pos_k = (logical[..., None] * page + jnp.arange(page)).reshape(db, N_HEADS, t, n_sel)
        s_sel = jnp.einsum('bhtd,bhtkd->bhtk', qh, ks, preferred_element_type=jnp.float32) * scale
        s_sel = s_sel - m4 * jnp.abs(pos_q[:, None] - pos_k).astype(jnp.float32)
        s_list.append(s_sel); v_list.append(vs); per_query.append(True)
    if n_own_past > 0:
        phys_own = page_table[:, nfb * ppb:]
        ko = cache_k[phys_own].reshape(db, n_own_past, N_KV_HEADS, HEAD_DIM)
        vo = cache_v[phys_own].reshape(db, n_own_past, N_KV_HEADS, HEAD_DIM)
        pos_k = nfb * MOBA_BLOCK + jnp.arange(n_own_past)
        s_op = jnp.einsum('bthd,bshd->bhts', q, ko, preferred_element_type=jnp.float32) * scale
        s_op = s_op - m4 * (pos_q[:, None] - pos_k[None, :]).astype(jnp.float32)
        s_list.append(s_op); v_list.append(vo); per_query.append(False)
    dist = pos_q[:, None] - pos_q[None, :]
    s_nw = jnp.einsum('bthd,bshd->bhts', q, k_new, preferred_element_type=jnp.float32) * scale
    s_nw = jnp.where(dist >= 0, s_nw - m4 * dist.astype(jnp.float32), -jnp.inf)
    s_list.append(s_nw); v_list.append(v_new); per_query.append(False)
    p = jax.nn.softmax(jnp.concatenate(s_list, axis=-1), axis=-1).astype(q.dtype)
    outs = []
    start = 0
    for s_part, vals, pq in zip(s_list, v_list, per_query):
        w = s_part.shape[-1]
        pp = p[..., start:start + w]
        start += w
        if pq:
            outs.append(jnp.einsum('bhtk,bhtkd->bhtd', pp, vals))
        else:
            outs.append(jnp.einsum('bhts,bshd->bhtd', pp, vals))
    out = outs[0]
    for o in outs[1:]:
        out = out + o
    return out.transpose(0, 2, 1, 3).reshape(db, t, N_HEADS * HEAD_DIM)


def setup_inputs(seed: int = 0) -> dict:
    key = jax.random.key(seed)
    ks = jax.random.split(key, 24)

    def nrm(k, shape, scale):
        return jax.random.normal(k, shape, jnp.float32) * scale

    n_pages = PAST_LEN // PAGE_SIZE
    n_used = DEC_BATCH * n_pages
    n_pool = n_used + max(1, n_used // 4)
    page_table = jax.random.permutation(ks[0], n_pool)[:n_used].reshape(DEC_BATCH, n_pages).astype(jnp.int32)
    return {
        'x_prompt': nrm(ks[1], (BATCH, SEQ, D_MODEL), 1.0),
        'x_sample': nrm(ks[2], (DEC_BATCH, DEC_SEQ, D_MODEL), 1.0),
        'cache_k': nrm(ks[3], (n_pool, PAGE_SIZE, N_KV_HEADS, HEAD_DIM), 1.0),
        'cache_v': nrm(ks[4], (n_pool, PAGE_SIZE, N_KV_HEADS, HEAD_DIM), 1.0),
        'page_table': page_table,
        'norm_mix_g': 1.0 + nrm(ks[5], (DEPTH, D_MODEL), 0.02),
        'norm_ffn_g': 1.0 + nrm(ks[6], (DEPTH, D_MODEL), 0.02),
        'gmlp_w_in': nrm(ks[7], (N_A_LAYERS, D_MODEL, 2 * D_GMLP), D_MODEL ** -0.5),
        'gmlp_ln_g': 1.0 + nrm(ks[8], (N_A_LAYERS, D_GMLP), 0.02),
        'gmlp_ln_b': nrm(ks[9], (N_A_LAYERS, D_GMLP), 0.02),
        'gmlp_w_s': nrm(ks[10], (N_A_LAYERS, N_SGU_GROUPS, CHUNK, CHUNK), CHUNK ** -0.5),
        'gmlp_b_s': 1.0 + nrm(ks[11], (N_A_LAYERS, N_SGU_GROUPS, CHUNK), 0.02),
        'gmlp_w_out': nrm(ks[12], (N_A_LAYERS, D_GMLP, D_MODEL), D_GMLP ** -0.5),
        'kv_norm_g': 1.0 + nrm(ks[13], (D_MODEL,), 0.02),
        'w_k': nrm(ks[14], (D_MODEL, KV_DIM), D_MODEL ** -0.5),
        'w_v': nrm(ks[15], (D_MODEL, KV_DIM), D_MODEL ** -0.5),
        'attn_w_q': nrm(ks[16], (N_B_LAYERS, D_MODEL, N_HEADS * HEAD_DIM), D_MODEL ** -0.5),
        'attn_w_o': nrm(ks[17], (N_B_LAYERS, N_HEADS * HEAD_DIM, D_MODEL), (N_HEADS * HEAD_DIM) ** -0.5),
        'ffn_w1': nrm(ks[18], (DEPTH, D_MODEL, D_FF), D_MODEL ** -0.5),
        'ffn_w2': nrm(ks[19], (DEPTH, D_FF, D_MODEL), D_FF ** -0.5),
        'final_norm_g': 1.0 + nrm(ks[20], (D_MODEL,), 0.02),
    }


def reference(x_prompt, x_sample, cache_k, cache_v, page_table, norm_mix_g, norm_ffn_g,
              gmlp_w_in, gmlp_ln_g, gmlp_ln_b, gmlp_w_s, gmlp_b_s, gmlp_w_out,
              kv_norm_g, w_k, w_v, attn_w_q, attn_w_o, ffn_w1, ffn_w2, final_norm_g):
    slopes = _alibi_slopes()
    xp, xs = x_prompt, x_sample
    bp, sp = xp.shape[:2]
    bs, ss = xs.shape[:2]
    tail = ((sp - 1) // CHUNK) * CHUNK
    sgu_p, sgu_s = [], []
    for layer in range(DEPTH):
        g = norm_mix_g[layer]
        if layer < N_A_LAYERS:
            xp, vp = _gmlp_layer(xp, g, gmlp_w_in[layer], gmlp_ln_g[layer], gmlp_ln_b[layer],
                                 gmlp_w_s[layer], gmlp_b_s[layer], gmlp_w_out[layer])
            xs, vs = _gmlp_layer(xs, g, gmlp_w_in[layer], gmlp_ln_g[layer], gmlp_ln_b[layer],
                                 gmlp_w_s[layer], gmlp_b_s[layer], gmlp_w_out[layer])
            sgu_p.append(vp[:, tail:])
            sgu_s.append(vs)
        else:
            if layer == N_A_LAYERS:
                k_prompt, v_prompt = _shared_kv(xp, kv_norm_g, w_k, w_v)
                k_sample, v_sample = _shared_kv(xs, kv_norm_g, w_k, w_v)
                kb_p, vb_p, kmean_p = _prompt_blocks(k_prompt, v_prompt)
                kmean_s = _cached_block_means(cache_k, page_table)
            b = layer - N_A_LAYERS
            qp = (_rmsnorm(xp, g) @ attn_w_q[b]).reshape(bp, sp, N_HEADS, HEAD_DIM)
            xp = xp + _moba_prompt(qp, kb_p, vb_p, kmean_p, slopes) @ attn_w_o[b]
            qs = (_rmsnorm(xs, g) @ attn_w_q[b]).reshape(bs, ss, N_HEADS, HEAD_DIM)
            xs = xs + _moba_sample(qs, k_sample, v_sample, cache_k, cache_v, page_table,
                                   kmean_s, slopes) @ attn_w_o[b]
        xp = _channel_mlp(xp, norm_ffn_g[layer], ffn_w1[layer], ffn_w2[layer])
        xs = _channel_mlp(xs, norm_ffn_g[layer], ffn_w1[layer], ffn_w2[layer])
    y_prompt = _rmsnorm(xp, final_norm_g)
    y_sample = _rmsnorm(xs, final_norm_g)
    sgu_v_prompt = jnp.stack(sgu_p)
    sgu_v_sample = jnp.stack(sgu_s)
    return (y_prompt, y_sample, k_prompt, v_prompt, k_sample, v_sample, sgu_v_prompt, sgu_v_sample)
```

```python
import functools

import jax
import jax.numpy as jnp
from jax import lax
from jax.experimental import pallas as pl
from jax.experimental.pallas import tpu as pltpu

D_MODEL = 2048
N_HEADS = 16
HEAD_DIM = 128
MOBA_BLOCK = 256
MOBA_TOP_K = 3
CHUNK = 128
D_GMLP = 2 * D_MODEL
N_SGU_GROUPS = 16
SGU_GROUP = D_GMLP // N_SGU_GROUPS
D_FF = 4 * D_MODEL
NORM_EPS = 1e-6

V7X_VMEM_LIMIT_BYTES = 56 * 1024 * 1024

F32 = jnp.float32
BF16 = jnp.bfloat16
NEG_INF = float("-inf")


def _params(semantics):
    return pltpu.CompilerParams(dimension_semantics=semantics,
                                vmem_limit_bytes=V7X_VMEM_LIMIT_BYTES)


def _rmsnorm_rows(x, g):
    return x * lax.rsqrt(jnp.mean(x * x, axis=-1, keepdims=True) + NORM_EPS) * g


def _dot(a, b):
    return jnp.dot(a, b, preferred_element_type=F32)


def _dot_nt(a, b, precision=None):
    return lax.dot_general(a, b, (((1,), (1,)), ((), ())), precision=precision,
                           preferred_element_type=F32)


def _norm_matmul_kernel(*refs, n_w):
    x_ref, g_ref = refs[:2]
    w_refs = refs[2:2 + n_w]
    o_refs = refs[2 + n_w:2 + 2 * n_w]
    xn_ref = refs[2 + 2 * n_w]

    @pl.when(pl.program_id(1) == 0)
    def _():
        xn_ref[...] = _rmsnorm_rows(x_ref[...], g_ref[...]).astype(BF16)

    xn = xn_ref[...]
    for w_ref, o_ref in zip(w_refs, o_refs):
        o_ref[...] = _dot(xn, w_ref[...])


def _norm_matmul(x, g, ws, *, tm, tn):
    rows, d = x.shape
    n = ws[0].shape[1]
    n_w = len(ws)
    return pl.pallas_call(
        functools.partial(_norm_matmul_kernel, n_w=n_w),
        grid=(rows // tm, n // tn),
        in_specs=[pl.BlockSpec((tm, d), lambda i, j: (i, 0)),
                  pl.BlockSpec((1, d), lambda i, j: (0, 0))]
                 + [pl.BlockSpec((d, tn), lambda i, j: (0, j)) for _ in ws],
        out_specs=[pl.BlockSpec((tm, tn), lambda i, j: (i, j)) for _ in ws],
        out_shape=[jax.ShapeDtypeStruct((rows, n), F32) for _ in ws],
        scratch_shapes=[pltpu.VMEM((tm, d), BF16)],
        compiler_params=_params(("parallel", "arbitrary")),
        name="norm_matmul",
    )(x, g.reshape(1, d), *ws)


def _proj_residual_kernel(a_ref, w_ref, x_ref, o_ref):
    o_ref[...] = x_ref[...] + _dot(a_ref[...].astype(BF16), w_ref[...])


def _proj_residual(a, w, x, *, tm, tn):
    rows, k = a.shape
    n = w.shape[1]
    return pl.pallas_call(
        _proj_residual_kernel,
        grid=(rows // tm, n // tn),
        in_specs=[pl.BlockSpec((tm, k), lambda i, j: (i, 0)),
                  pl.BlockSpec((k, tn), lambda i, j: (0, j)),
                  pl.BlockSpec((tm, tn), lambda i, j: (i, j))],
        out_specs=pl.BlockSpec((tm, tn), lambda i, j: (i, j)),
        out_shape=jax.ShapeDtypeStruct((rows, n), F32),
        compiler_params=_params(("parallel", "parallel")),
        name="proj_residual",
    )(a, w, x)


def _ffn_kernel(x_ref, g_ref, w1_ref, w2_ref, gf_ref, o_ref, xn_ref, *, final_norm):
    j = pl.program_id(1)

    @pl.when(j == 0)
    def _():
        x = x_ref[...]
        xn_ref[...] = _rmsnorm_rows(x, g_ref[...]).astype(BF16)
        o_ref[...] = x

    h = jnp.maximum(_dot(xn_ref[...], w1_ref[...]), 0.0)
    o_ref[...] += _dot((h * h).astype(BF16), w2_ref[...])

    if final_norm:
        @pl.when(j == pl.num_programs(1) - 1)
        def _():
            o_ref[...] = _rmsnorm_rows(o_ref[...], gf_ref[...])


def _ffn(x, g, w1, w2, gf, *, tm, tf, final_norm):
    rows, d = x.shape
    dff = w1.shape[1]
    return pl.pallas_call(
        functools.partial(_ffn_kernel, final_norm=final_norm),
        grid=(rows // tm, dff // tf),
        in_specs=[pl.BlockSpec((tm, d), lambda i, j: (i, 0)),
                  pl.BlockSpec((1, d), lambda i, j: (0, 0)),
                  pl.BlockSpec((d, tf), lambda i, j: (0, j)),
                  pl.BlockSpec((tf, d), lambda i, j: (j, 0)),
                  pl.BlockSpec((1, d), lambda i, j: (0, 0))],
        out_specs=pl.BlockSpec((tm, d), lambda i, j: (i, 0)),
        out_shape=jax.ShapeDtypeStruct((rows, d), F32),
        scratch_shapes=[pltpu.VMEM((tm, d), BF16)],
        compiler_params=_params(("parallel", "arbitrary")),
        name="ffn",
    )(x, g.reshape(1, d), w1, w2, gf.reshape(1, d))


def _gmlp_kernel(x_ref, g_ref, win_ref, lng_ref, lnb_ref, ws_ref, bs_ref, wout_ref,
                 y_ref, vout_ref, xn_ref, u_ref, v_ref, wsb_ref,
                 *, n1, tn, tk, chunk_len, mix_rows, v_period):
    i = pl.program_id(0)
    s = pl.program_id(1)
    tm = x_ref.shape[0]
    half = n1 // 2

    @pl.when(s == 0)
    def _():
        xn_ref[...] = _rmsnorm_rows(x_ref[...], g_ref[...]).astype(BF16)

    @pl.when(s < n1)
    def _():
        z = jax.nn.gelu(_dot(xn_ref[...], win_ref[...]))

        @pl.when(s < half)
        def _():
            u_ref[:, pl.ds(pl.multiple_of(s * tn, tn), tn)] = z

        @pl.when(s >= half)
        def _():
            v_ref[:, pl.ds(pl.multiple_of((s - half) * tn, tn), tn)] = z

    @pl.when(s == n1)
    def _():
        r = lax.broadcasted_iota(jnp.int32, (mix_rows, mix_rows), 0)
        c = lax.broadcasted_iota(jnp.int32, (mix_rows, mix_rows), 1)
        keep = (r // chunk_len == c // chunk_len) & (c <= r)
        for grp in range(N_SGU_GROUPS):
            wsb_ref[grp] = jnp.where(keep, ws_ref[grp], 0.0).astype(BF16)
        ln_g = lng_ref[...]
        ln_b = lnb_ref[...]

        def mix_tile(t, carry):
            rows = pl.ds(pl.multiple_of(t * mix_rows, mix_rows), mix_rows)
            v = v_ref[rows, :]
            mu = jnp.mean(v, axis=-1, keepdims=True)
            vc = v - mu
            var = jnp.mean(vc * vc, axis=-1, keepdims=True)
            vn = vc * lax.rsqrt(var + NORM_EPS) * ln_g + ln_b
            v_ref[rows, :] = vn
            for grp in range(N_SGU_GROUPS):
                cols = slice(grp * SGU_GROUP, (grp + 1) * SGU_GROUP)
                mixed = _dot(wsb_ref[grp], vn[:, cols].astype(BF16)) + bs_ref[:, grp:grp + 1]
                u_ref[rows, cols] = u_ref[rows, cols] * mixed
            return carry

        lax.fori_loop(0, tm // mix_rows, mix_tile, 0)
        y_ref[...] = x_ref[...]

        @pl.when(i % v_period == v_period - 1)
        def _():
            vout_ref[...] = v_ref[tm - vout_ref.shape[0]:, :]

    @pl.when(s >= n1)
    def _():
        k = s - n1
        h = u_ref[:, pl.ds(pl.multiple_of(k * tk, tk), tk)].astype(BF16)
        y_ref[...] += _dot(h, wout_ref[...])


def _gmlp(x, g, w_in, ln_g, ln_b, ws_tiled, bs_tiled, w_out, *, tm, tn, tk, chunk_len,
          v_rows, v_period):
    rows, d = x.shape
    mix_rows = ws_tiled.shape[1]
    n1 = (2 * D_GMLP) // tn
    n2 = D_GMLP // tk
    n_vblocks = rows // (tm * v_period)
    kern = functools.partial(_gmlp_kernel, n1=n1, tn=tn, tk=tk, chunk_len=chunk_len,
                             mix_rows=mix_rows, v_period=v_period)
    return pl.pallas_call(
        kern,
        grid=(rows // tm, n1 + n2),
        in_specs=[pl.BlockSpec((tm, d), lambda i, s: (i, 0)),
                  pl.BlockSpec((1, d), lambda i, s: (0, 0)),
                  pl.BlockSpec((d, tn), lambda i, s: (0, jnp.minimum(s, n1 - 1))),
                  pl.BlockSpec((1, D_GMLP), lambda i, s: (0, 0)),
                  pl.BlockSpec((1, D_GMLP), lambda i, s: (0, 0)),
                  pl.BlockSpec((N_SGU_GROUPS, mix_rows, mix_rows), lambda i, s: (0, 0, 0)),
                  pl.BlockSpec((mix_rows, N_SGU_GROUPS), lambda i, s: (0, 0)),
                  pl.BlockSpec((tk, d), lambda i, s: (jnp.maximum(s - n1, 0), 0))],
        out_specs=[pl.BlockSpec((tm, d), lambda i, s: (i, 0)),
                   pl.BlockSpec((v_rows, D_GMLP), lambda i, s: (i // v_period, 0))],
        out_shape=[jax.ShapeDtypeStruct((rows, d), F32),
                   jax.ShapeDtypeStruct((n_vblocks * v_rows, D_GMLP), F32)],
        scratch_shapes=[pltpu.VMEM((tm, d), BF16),
                        pltpu.VMEM((tm, D_GMLP), F32),
                        pltpu.VMEM((tm, D_GMLP), F32),
                        pltpu.VMEM((N_SGU_GROUPS, mix_rows, mix_rows), BF16)],
        compiler_params=_params(("arbitrary", "arbitrary")),
        name="gmlp",
    )(x, g.reshape(1, d), w_in, ln_g.reshape(1, D_GMLP), ln_b.reshape(1, D_GMLP),
      ws_tiled, bs_tiled, w_out)


def _top_blocks(gate, blk, n_take):
    nb = gate.shape[1]
    picks = []
    sel = jnp.zeros_like(gate)
    for r in range(MOBA_TOP_K):
        m = jnp.max(gate, axis=1, keepdims=True)
        idx = jnp.min(jnp.where(gate == m, blk, float(nb)), axis=1, keepdims=True)
        hit = blk == idx
        picks.append(idx)
        counts = 1.0 if n_take is None else jnp.where(r < n_take, 1.0, 0.0)
        sel = jnp.maximum(sel, jnp.where(hit, counts, 0.0))
        gate = jnp.where(hit, NEG_INF, gate)
    return picks, sel


def _moba_prompt_kernel(slopes_ref, q_ref, k_ref, v_ref, o_ref, kb_ref, vb_ref, kmean_ref):
    h = pl.program_id(1)
    qt = pl.program_id(2)
    n_blocks = kb_ref.shape[0] // MOBA_BLOCK
    scale = HEAD_DIM ** -0.5
    slope = slopes_ref[h]

    @pl.when(qt == 0)
    def _():
        kf = k_ref[0]
        kb_ref[...] = kf.astype(BF16)
        vb_ref[...] = v_ref[0].astype(BF16)
        kmean_ref[...] = jnp.mean(kf.reshape(n_blocks, MOBA_BLOCK, HEAD_DIM), axis=1)

    q = q_ref[0]
    qb = q.astype(BF16)
    gate = _dot_nt(q, kmean_ref[...], precision=lax.Precision.HIGHEST)
    blk = lax.broadcasted_iota(jnp.int32, gate.shape, 1)
    gate = jnp.where(blk < qt, gate, NEG_INF)
    blk_f = blk.astype(F32)
    _, sel = _top_blocks(gate, blk_f, qt)

    row = lax.broadcasted_iota(jnp.int32, (MOBA_BLOCK, MOBA_BLOCK), 0)
    col = lax.broadcasted_iota(jnp.int32, (MOBA_BLOCK, MOBA_BLOCK), 1)
    rel = row - col

    own = pl.ds(pl.multiple_of(qt * MOBA_BLOCK, MOBA_BLOCK), MOBA_BLOCK)
    s = _dot_nt(qb, kb_ref[own, :]) * scale
    s = jnp.where(rel >= 0, s - slope * rel.astype(F32), NEG_INF)
    m0 = jnp.max(s, axis=1, keepdims=True)
    p = jnp.exp(s - m0)
    l0 = jnp.sum(p, axis=1, keepdims=True)
    acc0 = _dot(p.astype(BF16), vb_ref[own, :])

    def past_block(j, carry):
        m, l, acc = carry
        rows = pl.ds(pl.multiple_of(j * MOBA_BLOCK, MOBA_BLOCK), MOBA_BLOCK)
        s = _dot_nt(qb, kb_ref[rows, :]) * scale
        dist = (rel + (qt - j) * MOBA_BLOCK).astype(F32)
        picked = jnp.max(jnp.where(blk == j, sel, 0.0), axis=1, keepdims=True)
        s = jnp.where(picked > 0.0, s - slope * dist, NEG_INF)
        m_new = jnp.maximum(m, jnp.max(s, axis=1, keepdims=True))
        alpha = jnp.exp(m - m_new)
        p = jnp.exp(s - m_new)
        l = alpha * l + jnp.sum(p, axis=1, keepdims=True)
        acc = alpha * acc + _dot(p.astype(BF16), vb_ref[rows, :])
        return m_new, l, acc

    _, l, acc = lax.fori_loop(0, qt, past_block, (m0, l0, acc0))
    o_ref[0] = (acc / l).astype(o_ref.dtype)


def _moba_prompt(q, k, v, slopes):
    bsz, seq, _ = q.shape
    n_qt = seq // MOBA_BLOCK
    return pl.pallas_call(
        _moba_prompt_kernel,
        grid_spec=pltpu.PrefetchScalarGridSpec(
            num_scalar_prefetch=0,
            grid=(bsz, N_HEADS, n_qt),
            in_specs=[pl.BlockSpec(memory_space=pltpu.SMEM),
                      pl.BlockSpec((1, MOBA_BLOCK, HEAD_DIM), lambda b, h, t: (b, t, h)),
                      pl.BlockSpec((1, seq, HEAD_DIM), lambda b, h, t: (b, 0, h)),
                      pl.BlockSpec((1, seq, HEAD_DIM), lambda b, h, t: (b, 0, h))],
            out_specs=pl.BlockSpec((1, MOBA_BLOCK, HEAD_DIM), lambda b, h, t: (b, t, h)),
            scratch_shapes=[pltpu.VMEM((seq, HEAD_DIM), BF16),
                            pltpu.VMEM((seq, HEAD_DIM), BF16),
                            pltpu.VMEM((seq // MOBA_BLOCK, HEAD_DIM), F32)]),
        out_shape=jax.ShapeDtypeStruct(q.shape, BF16),
        compiler_params=_params(("parallel", "parallel", "arbitrary")),
        name="moba_prompt",
    )(slopes, q, k, v)


def _cached_means_kernel(pt_ref, k0_ref, k1_ref, o_ref):
    del pt_ref
    total = (jnp.sum(k0_ref[0], axis=0, keepdims=True)
             + jnp.sum(k1_ref[0], axis=0, keepdims=True))
    o_ref[0, 0] = total / MOBA_BLOCK


def _cached_means(cache_k3, pt_flat, db, n_pages):
    page = cache_k3.shape[1]
    kv_dim = cache_k3.shape[2]
    ppb = MOBA_BLOCK // page
    assert ppb == 2
    nfb = (n_pages * page) // MOBA_BLOCK

    def page_spec(p):
        return pl.BlockSpec((1, page, kv_dim),
                            lambda b, n, pt: (pt[b * n_pages + n * ppb + p], 0, 0))

    out = pl.pallas_call(
        _cached_means_kernel,
        grid_spec=pltpu.PrefetchScalarGridSpec(
            num_scalar_prefetch=1,
            grid=(db, nfb),
            in_specs=[page_spec(0), page_spec(1)],
            out_specs=pl.BlockSpec((1, 1, 1, kv_dim), lambda b, n, pt: (b, n, 0, 0))),
        out_shape=jax.ShapeDtypeStruct((db, nfb, 1, kv_dim), F32),
        compiler_params=_params(("parallel", "parallel")),
        name="cached_means",
    )(pt_flat, cache_k3, cache_k3)
    return out.reshape(db, nfb, kv_dim)


def _sample_topk_kernel(q_ref, km_ref, o_ref):
    t = q_ref.shape[1]
    lane = lax.broadcasted_iota(jnp.int32, (t, 128), 1)
    for h in range(N_HEADS):
        cols = slice(h * HEAD_DIM, (h + 1) * HEAD_DIM)
        gate = _dot_nt(q_ref[0, :, cols], km_ref[0, :, cols], precision=lax.Precision.HIGHEST)
        blk_f = lax.broadcasted_iota(jnp.int32, gate.shape, 1).astype(F32)
        picks, _ = _top_blocks(gate, blk_f, None)
        out = jnp.zeros((t, 128), F32)
        for r, idx in enumerate(picks):
            out = jnp.where(lane == r, idx, out)
        o_ref[0, h] = out.astype(jnp.int32)


def _sample_topk(q, kmean):
    db, t, _ = q.shape
    nfb = kmean.shape[1]
    assert nfb >= MOBA_TOP_K
    out = pl.pallas_call(
        _sample_topk_kernel,
        grid=(db,),
        in_specs=[pl.BlockSpec((1, t, D_MODEL), lambda b: (b, 0, 0)),
                  pl.BlockSpec((1, nfb, D_MODEL), lambda b: (b, 0, 0))],
        out_specs=pl.BlockSpec((1, N_HEADS, t, 128), lambda b: (b, 0, 0, 0)),
        out_shape=jax.ShapeDtypeStruct((db, N_HEADS, t, 128), jnp.int32),
        compiler_params=_params(("parallel",)),
        name="sample_topk",
    )(q, kmean)
    return out[..., :MOBA_TOP_K]


def _moba_sample_kernel(idx_ref, pt_ref, slopes_ref, q_ref, kn_ref, vn_ref, *rest,
                        n_sel_pages, ppb, past_len):
    del pt_ref
    k_refs = rest[:n_sel_pages]
    v_refs = rest[n_sel_pages:2 * n_sel_pages]
    o_ref = rest[2 * n_sel_pages]
    b = pl.program_id(0)
    h = pl.program_id(1)
    t = pl.program_id(2)
    n_t = q_ref.shape[1]
    page = k_refs[0].shape[1]
    scale = HEAD_DIM ** -0.5
    slope = slopes_ref[h]
    base = ((b * N_HEADS + h) * n_t + t) * MOBA_TOP_K

    q = q_ref[0, pl.ds(t, 1), :]
    pos_q = past_len + t
    off = lax.broadcasted_iota(jnp.int32, (page, 1), 0)

    scores = []
    for n in range(n_sel_pages):
        logical = idx_ref[base + n // ppb] * ppb + (n % ppb)
        s = jnp.sum(k_refs[n][0] * q, axis=1, keepdims=True) * scale
        dist = jnp.abs(pos_q - (logical * page + off)).astype(F32)
        scores.append(s - slope * dist)
    tn = lax.broadcasted_iota(jnp.int32, (n_t, 1), 0)
    s_new = jnp.sum(kn_ref[0] * q, axis=1, keepdims=True) * scale
    d_new = t - tn
    s_new = jnp.where(d_new >= 0, s_new - slope * d_new.astype(F32), NEG_INF)

    m = jnp.max(s_new, axis=0, keepdims=True)
    for s in scores:
        m = jnp.maximum(m, jnp.max(s, axis=0, keepdims=True))
    p_new = jnp.exp(s_new - m)
    l = jnp.sum(p_new, axis=0, keepdims=True)
    acc = jnp.sum(p_new * vn_ref[0], axis=0, keepdims=True)
    for n, s in enumerate(scores):
        p = jnp.exp(s - m)
        l = l + jnp.sum(p, axis=0, keepdims=True)
        acc = acc + jnp.sum(p * v_refs[n][0], axis=0, keepdims=True)
    o_ref[0, pl.ds(t, 1), :] = acc / l


def _moba_sample(q, k_new, v_new, cache_k3, cache_v3, idx_flat, pt_flat, slopes, n_pages):
    db, t, _ = q.shape
    page = cache_k3.shape[1]
    ppb = MOBA_BLOCK // page
    n_sel_pages = MOBA_TOP_K * ppb

    def new_spec():
        return pl.BlockSpec((1, t, HEAD_DIM), lambda b, h, ti, idx, pt: (b, 0, h))

    def page_spec(n):
        def index_map(b, h, ti, idx, pt):
            logical = idx[((b * N_HEADS + h) * t + ti) * MOBA_TOP_K + n // ppb] * ppb + n % ppb
            return (pt[b * n_pages + logical], 0, h)
        return pl.BlockSpec((1, page, HEAD_DIM), index_map)

    kern = functools.partial(_moba_sample_kernel, n_sel_pages=n_sel_pages, ppb=ppb,
                             past_len=n_pages * page)
    return pl.pallas_call(
        kern,
        grid_spec=pltpu.PrefetchScalarGridSpec(
            num_scalar_prefetch=2,
            grid=(db, N_HEADS, t),
            in_specs=[pl.BlockSpec(memory_space=pltpu.SMEM), new_spec(), new_spec(), new_spec()]
                     + [page_spec(n) for n in range(n_sel_pages)]
                     + [page_spec(n) for n in range(n_sel_pages)],
            out_specs=pl.BlockSpec((1, t, HEAD_DIM), lambda b, h, ti, idx, pt: (b, 0, h))),
        out_shape=jax.ShapeDtypeStruct(q.shape, F32),
        compiler_params=_params(("parallel", "parallel", "arbitrary")),
        name="moba_sample",
    )(idx_flat, pt_flat, slopes, q, k_new, v_new,
      *([cache_k3] * n_sel_pages), *([cache_v3] * n_sel_pages))


def kernel(x_prompt, x_sample, cache_k, cache_v, page_table, norm_mix_g, norm_ffn_g,
           gmlp_w_in, gmlp_ln_g, gmlp_ln_b, gmlp_w_s, gmlp_b_s, gmlp_w_out,
           kv_norm_g, w_k, w_v, attn_w_q, attn_w_o, ffn_w1, ffn_w2, final_norm_g):
    bp, sp, d = x_prompt.shape
    bs, ss, _ = x_sample.shape
    depth = norm_mix_g.shape[0]
    n_a = gmlp_w_in.shape[0]
    n_pool, page, n_kv, dh = cache_k.shape
    n_pages = page_table.shape[1]
    assert sp % MOBA_BLOCK == 0 and sp % CHUNK == 0 and ss <= CHUNK
    assert (n_pages * page) % MOBA_BLOCK == 0

    slopes = jnp.exp2(-8.0 * jnp.arange(1, N_HEADS + 1, dtype=F32) / N_HEADS)
    to_bf16 = lambda w: w.astype(BF16)
    w_in_b, w_out_b = to_bf16(gmlp_w_in), to_bf16(gmlp_w_out)
    w_k_b, w_v_b = to_bf16(w_k), to_bf16(w_v)
    w_q_b, w_o_b = to_bf16(attn_w_q), to_bf16(attn_w_o)
    w1_b, w2_b = to_bf16(ffn_w1), to_bf16(ffn_w2)

    rows_p, rows_s = bp * sp, bs * ss
    xp = x_prompt.reshape(rows_p, d)
    xs = x_sample.reshape(rows_s, d)
    cache_k3 = cache_k.reshape(n_pool, page, n_kv * dh)
    cache_v3 = cache_v.reshape(n_pool, page, n_kv * dh)
    pt_flat = page_table.reshape(-1)

    tm_p = 512
    sgu_p, sgu_s = [], []
    for layer in range(depth):
        g = norm_mix_g[layer]
        if layer < n_a:
            ws_p = gmlp_w_s[layer]
            bs_p = gmlp_b_s[layer].T
            ws_s = jnp.tile(gmlp_w_s[layer][:, :ss, :ss], (1, bs, bs))
            bs_s = jnp.tile(gmlp_b_s[layer][:, :ss].T, (bs, 1))
            xp, vp = _gmlp(xp, g, w_in_b[layer], gmlp_ln_g[layer], gmlp_ln_b[layer], ws_p, bs_p,
                           w_out_b[layer], tm=tm_p, tn=512, tk=512, chunk_len=CHUNK,
                           v_rows=CHUNK, v_period=sp // tm_p)
            xs, vs = _gmlp(xs, g, w_in_b[layer], gmlp_ln_g[layer], gmlp_ln_b[layer], ws_s, bs_s,
                           w_out_b[layer], tm=rows_s, tn=512, tk=512, chunk_len=ss,
                           v_rows=rows_s, v_period=1)
            sgu_p.append(vp.reshape(bp, CHUNK, D_GMLP))
            sgu_s.append(vs.reshape(bs, ss, D_GMLP))
        else:
            if layer == n_a:
                kp, vp_ = _norm_matmul(xp, kv_norm_g, [w_k_b, w_v_b], tm=tm_p, tn=512)
                ks_, vs_ = _norm_matmul(xs, kv_norm_g, [w_k_b, w_v_b], tm=rows_s, tn=512)
                kmean_s = _cached_means(cache_k3, pt_flat, bs, n_pages)
            a = layer - n_a
            (qp,) = _norm_matmul(xp, g, [w_q_b[a]], tm=tm_p, tn=512)
            att_p = _moba_prompt(qp.reshape(bp, sp, d), kp.reshape(bp, sp, d),
                                 vp_.reshape(bp, sp, d), slopes)
            xp = _proj_residual(att_p.reshape(rows_p, d), w_o_b[a], xp, tm=tm_p, tn=512)

            (qs,) = _norm_matmul(xs, g, [w_q_b[a]], tm=rows_s, tn=512)
            qs3 = qs.reshape(bs, ss, d)
            idx = _sample_topk(qs3, kmean_s)
            att_s = _moba_sample(qs3, ks_.reshape(bs, ss, d), vs_.reshape(bs, ss, d),
                                 cache_k3, cache_v3, idx.reshape(-1), pt_flat, slopes, n_pages)
            xs = _proj_residual(att_s.reshape(rows_s, d), w_o_b[a], xs, tm=rows_s, tn=512)
        last = layer == depth - 1
        xp = _ffn(xp, norm_ffn_g[layer], w1_b[layer], w2_b[layer], final_norm_g,
                  tm=tm_p, tf=512, final_norm=last)
        xs = _ffn(xs, norm_ffn_g[layer], w1_b[layer], w2_b[layer], final_norm_g,
                  tm=rows_s, tf=512, final_norm=last)

    kv_shape_p = (bp, sp, n_kv, dh)
    kv_shape_s = (bs, ss, n_kv, dh)
    return (xp.reshape(bp, sp, d), xs.reshape(bs, ss, d),
            kp.reshape(kv_shape_p), vp_.reshape(kv_shape_p),
            ks_.reshape(kv_shape_s), vs_.reshape(kv_shape_s),
            jnp.stack(sgu_p), jnp.stack(sgu_s))
```

```python
import functools

import jax
import jax.numpy as jnp
from jax import lax
from jax.experimental import pallas as pl
from jax.experimental.pallas import tpu as pltpu

D_MODEL = 2048
N_HEADS = 16
HEAD_DIM = 128
MOBA_BLOCK = 256
MOBA_TOP_K = 3
CHUNK = 128
D_GMLP = 2 * D_MODEL
N_SGU_GROUPS = 16
SGU_GROUP = D_GMLP // N_SGU_GROUPS
D_FF = 4 * D_MODEL
NORM_EPS = 1e-6

V7X_VMEM_LIMIT_BYTES = 56 * 1024 * 1024

F32 = jnp.float32
BF16 = jnp.bfloat16
NEG_INF = float("-inf")
LOG2E = 1.4426950408889634


def _params(semantics):
    return pltpu.CompilerParams(dimension_semantics=semantics,
                                vmem_limit_bytes=V7X_VMEM_LIMIT_BYTES)


def _rmsnorm_rows(x, g):
    return x * lax.rsqrt(jnp.mean(x * x, axis=-1, keepdims=True) + NORM_EPS) * g


def _dot(a, b):
    return jnp.dot(a, b, preferred_element_type=F32)


def _dot_nt(a, b, precision=None):
    return lax.dot_general(a, b, (((1,), (1,)), ((), ())), precision=precision,
                           preferred_element_type=F32)


def _norm_matmul_kernel(*refs, n_w):
    x_ref, g_ref = refs[:2]
    w_refs = refs[2:2 + n_w]
    o_refs = refs[2 + n_w:2 + 2 * n_w]
    xn_ref = refs[2 + 2 * n_w]

    @pl.when(pl.program_id(1) == 0)
    def _():
        xn_ref[...] = _rmsnorm_rows(x_ref[...], g_ref[...]).astype(BF16)

    xn = xn_ref[...]
    for w_ref, o_ref in zip(w_refs, o_refs):
        o_ref[...] = _dot(xn, w_ref[...])


def _norm_matmul(x, g, ws, *, tm, tn):
    rows, d = x.shape
    n = ws[0].shape[1]
    n_w = len(ws)
    return pl.pallas_call(
        functools.partial(_norm_matmul_kernel, n_w=n_w),
        grid=(rows // tm, n // tn),
        in_specs=[pl.BlockSpec((tm, d), lambda i, j: (i, 0)),
                  pl.BlockSpec((1, d), lambda i, j: (0, 0))]
                 + [pl.BlockSpec((d, tn), lambda i, j: (0, j)) for _ in ws],
        out_specs=[pl.BlockSpec((tm, tn), lambda i, j: (i, j)) for _ in ws],
        out_shape=[jax.ShapeDtypeStruct((rows, n), F32) for _ in ws],
        scratch_shapes=[pltpu.VMEM((tm, d), BF16)],
        compiler_params=_params(("parallel", "arbitrary")),
        name="norm_matmul",
    )(x, g.reshape(1, d), *ws)


def _proj_residual_kernel(a_ref, w_ref, x_ref, o_ref):
    o_ref[...] = x_ref[...] + _dot(a_ref[...].astype(BF16), w_ref[...])


def _proj_residual(a, w, x, *, tm, tn):
    rows, k = a.shape
    n = w.shape[1]
    return pl.pallas_call(
        _proj_residual_kernel,
        grid=(rows // tm, n // tn),
        in_specs=[pl.BlockSpec((tm, k), lambda i, j: (i, 0)),
                  pl.BlockSpec((k, tn), lambda i, j: (0, j)),
                  pl.BlockSpec((tm, tn), lambda i, j: (i, j))],
        out_specs=pl.BlockSpec((tm, tn), lambda i, j: (i, j)),
        out_shape=jax.ShapeDtypeStruct((rows, n), F32),
        compiler_params=_params(("parallel", "parallel")),
        name="proj_residual",
    )(a, w, x)


def _ffn_kernel(x_ref, g_ref, w1_ref, w2_ref, gf_ref, o_ref, xn_ref, *, final_norm):
    j = pl.program_id(1)

    @pl.when(j == 0)
    def _():
        x = x_ref[...]
        xn_ref[...] = _rmsnorm_rows(x, g_ref[...]).astype(BF16)
        o_ref[...] = x

    h = jnp.maximum(_dot(xn_ref[...], w1_ref[...]), 0.0)
    o_ref[...] += _dot((h * h).astype(BF16), w2_ref[...])

    if final_norm:
        @pl.when(j == pl.num_programs(1) - 1)
        def _():
            o_ref[...] = _rmsnorm_rows(o_ref[...], gf_ref[...])


def _ffn(x, g, w1, w2, gf, *, tm, tf, final_norm):
    rows, d = x.shape
    dff = w1.shape[1]
    return pl.pallas_call(
        functools.partial(_ffn_kernel, final_norm=final_norm),
        grid=(rows // tm, dff // tf),
        in_specs=[pl.BlockSpec((tm, d), lambda i, j: (i, 0)),
                  pl.BlockSpec((1, d), lambda i, j: (0, 0)),
                  pl.BlockSpec((d, tf), lambda i, j: (0, j)),
                  pl.BlockSpec((tf, d), lambda i, j: (j, 0)),
                  pl.BlockSpec((1, d), lambda i, j: (0, 0))],
        out_specs=pl.BlockSpec((tm, d), lambda i, j: (i, 0)),
        out_shape=jax.ShapeDtypeStruct((rows, d), F32),
        scratch_shapes=[pltpu.VMEM((tm, d), BF16)],
        compiler_params=_params(("parallel", "arbitrary")),
        name="ffn",
    )(x, g.reshape(1, d), w1, w2, gf.reshape(1, d))


def _gmlp_kernel(x_ref, g_ref, win_ref, lng_ref, lnb_ref, ws_ref, bs_ref, wout_ref,
                 y_ref, vout_ref, xn_ref, u_ref, v_ref, wsb_ref,
                 *, n1, tn, tk, chunk_len, mix_rows, v_period):
    i = pl.program_id(0)
    s = pl.program_id(1)
    tm = x_ref.shape[0]
    half = n1 // 2

    @pl.when(s == 0)
    def _():
        xn_ref[...] = _rmsnorm_rows(x_ref[...], g_ref[...]).astype(BF16)

    @pl.when(s < n1)
    def _():
        z = jax.nn.gelu(_dot(xn_ref[...], win_ref[...]))

        @pl.when(s < half)
        def _():
            u_ref[:, pl.ds(pl.multiple_of(s * tn, tn), tn)] = z

        @pl.when(s >= half)
        def _():
            v_ref[:, pl.ds(pl.multiple_of((s - half) * tn, tn), tn)] = z

    @pl.when(s == n1)
    def _():
        r = lax.broadcasted_iota(jnp.int32, (mix_rows, mix_rows), 0)
        c = lax.broadcasted_iota(jnp.int32, (mix_rows, mix_rows), 1)
        keep = (r // chunk_len == c // chunk_len) & (c <= r)
        for grp in range(N_SGU_GROUPS):
            wsb_ref[grp] = jnp.where(keep, ws_ref[grp], 0.0).astype(BF16)
        ln_g = lng_ref[...]
        ln_b = lnb_ref[...]

        def mix_tile(t, carry):
            rows = pl.ds(pl.multiple_of(t * mix_rows, mix_rows), mix_rows)
            v = v_ref[rows, :]
            mu = jnp.mean(v, axis=-1, keepdims=True)
            vc = v - mu
            var = jnp.mean(vc * vc, axis=-1, keepdims=True)
            vn = vc * lax.rsqrt(var + NORM_EPS) * ln_g + ln_b
            v_ref[rows, :] = vn
            for grp in range(N_SGU_GROUPS):
                cols = slice(grp * SGU_GROUP, (grp + 1) * SGU_GROUP)
                mixed = _dot(wsb_ref[grp], vn[:, cols].astype(BF16)) + bs_ref[:, grp:grp + 1]
                u_ref[rows, cols] = u_ref[rows, cols] * mixed
            return carry

        lax.fori_loop(0, tm // mix_rows, mix_tile, 0)
        y_ref[...] = x_ref[...]

        @pl.when(i % v_period == v_period - 1)
        def _():
            vout_ref[...] = v_ref[tm - vout_ref.shape[0]:, :]

    @pl.when(s >= n1)
    def _():
        k = s - n1
        h = u_ref[:, pl.ds(pl.multiple_of(k * tk, tk), tk)].astype(BF16)
        y_ref[...] += _dot(h, wout_ref[...])


def _gmlp(x, g, w_in, ln_g, ln_b, ws_tiled, bs_tiled, w_out, *, tm, tn, tk, chunk_len,
          v_rows, v_period):
    rows, d = x.shape
    mix_rows = ws_tiled.shape[1]
    n1 = (2 * D_GMLP) // tn
    n2 = D_GMLP // tk
    n_vblocks = rows // (tm * v_period)
    kern = functools.partial(_gmlp_kernel, n1=n1, tn=tn, tk=tk, chunk_len=chunk_len,
                             mix_rows=mix_rows, v_period=v_period)
    return pl.pallas_call(
        kern,
        grid=(rows // tm, n1 + n2),
        in_specs=[pl.BlockSpec((tm, d), lambda i, s: (i, 0)),
                  pl.BlockSpec((1, d), lambda i, s: (0, 0)),
                  pl.BlockSpec((d, tn), lambda i, s: (0, jnp.minimum(s, n1 - 1))),
                  pl.BlockSpec((1, D_GMLP), lambda i, s: (0, 0)),
                  pl.BlockSpec((1, D_GMLP), lambda i, s: (0, 0)),
                  pl.BlockSpec((N_SGU_GROUPS, mix_rows, mix_rows), lambda i, s: (0, 0, 0)),
                  pl.BlockSpec((mix_rows, N_SGU_GROUPS), lambda i, s: (0, 0)),
                  pl.BlockSpec((tk, d), lambda i, s: (jnp.maximum(s - n1, 0), 0))],
        out_specs=[pl.BlockSpec((tm, d), lambda i, s: (i, 0)),
                   pl.BlockSpec((v_rows, D_GMLP), lambda i, s: (i // v_period, 0))],
        out_shape=[jax.ShapeDtypeStruct((rows, d), F32),
                   jax.ShapeDtypeStruct((n_vblocks * v_rows, D_GMLP), F32)],
        scratch_shapes=[pltpu.VMEM((tm, d), BF16),
                        pltpu.VMEM((tm, D_GMLP), F32),
                        pltpu.VMEM((tm, D_GMLP), F32),
                        pltpu.VMEM((N_SGU_GROUPS, mix_rows, mix_rows), BF16)],
        compiler_params=_params(("arbitrary", "arbitrary")),
        name="gmlp",
    )(x, g.reshape(1, d), w_in, ln_g.reshape(1, D_GMLP), ln_b.reshape(1, D_GMLP),
      ws_tiled, bs_tiled, w_out)


def _top_blocks(gate, blk, n_take, axis):
    nb = gate.shape[axis]
    picks = []
    sel = jnp.zeros_like(gate)
    for r in range(MOBA_TOP_K):
        m = jnp.max(gate, axis=axis, keepdims=True)
        idx = jnp.min(jnp.where(gate == m, blk, float(nb)), axis=axis, keepdims=True)
        hit = blk == idx
        picks.append(idx)
        counts = 1.0 if n_take is None else jnp.where(r < n_take, 1.0, 0.0)
        sel = jnp.maximum(sel, jnp.where(hit, counts, 0.0))
        gate = jnp.where(hit, NEG_INF, gate)
    return picks, sel


def _kv_proj_kernel(x_ref, g_ref, wk_ref, wv_ref, k_ref, v_ref, kb_ref, vb_ref, kmean_ref, xn_ref):
    @pl.when(pl.program_id(1) == 0)
    def _():
        xn_ref[...] = _rmsnorm_rows(x_ref[...], g_ref[...]).astype(BF16)

    xn = xn_ref[...]
    k = _dot(xn, wk_ref[...])
    v = _dot(xn, wv_ref[...])
    k_ref[...] = k
    v_ref[...] = v
    kb_ref[...] = k.astype(BF16)
    vb_ref[...] = v.astype(BF16)
    for blk in range(kmean_ref.shape[0]):
        rows = slice(blk * MOBA_BLOCK, (blk + 1) * MOBA_BLOCK)
        kmean_ref[blk] = jnp.mean(k[rows, :], axis=0, keepdims=True)


def _kv_proj(x, g, wk, wv, *, tm, tn):
    rows, d = x.shape
    n = wk.shape[1]
    blocks_per_tile = tm // MOBA_BLOCK
    tile = lambda: pl.BlockSpec((tm, tn), lambda i, j: (i, j))
    return pl.pallas_call(
        _kv_proj_kernel,
        grid=(rows // tm, n // tn),
        in_specs=[pl.BlockSpec((tm, d), lambda i, j: (i, 0)),
                  pl.BlockSpec((1, d), lambda i, j: (0, 0)),
                  pl.BlockSpec((d, tn), lambda i, j: (0, j)),
                  pl.BlockSpec((d, tn), lambda i, j: (0, j))],
        out_specs=[tile(), tile(), tile(), tile(),
                   pl.BlockSpec((blocks_per_tile, 1, tn), lambda i, j: (i, 0, j))],
        out_shape=[jax.ShapeDtypeStruct((rows, n), F32), jax.ShapeDtypeStruct((rows, n), F32),
                   jax.ShapeDtypeStruct((rows, n), BF16), jax.ShapeDtypeStruct((rows, n), BF16),
                   jax.ShapeDtypeStruct((rows // MOBA_BLOCK, 1, n), F32)],
        scratch_shapes=[pltpu.VMEM((tm, d), BF16)],
        compiler_params=_params(("parallel", "arbitrary")),
        name="kv_proj",
    )(x, g.reshape(1, d), wk, wv)


HEADS_PER_STEP = 8


def _moba_prompt_kernel(slopes_ref, q_ref, kb_ref, vb_ref, kmean_ref, o_ref,
                        vt_ref, qbt_ref, bias_ref, sel_ref, acc_ref):
    hg = pl.program_id(1)
    qt = pl.program_id(2)
    n_blocks = kb_ref.shape[1] // MOBA_BLOCK
    score_scale = (HEAD_DIM ** -0.5) * LOG2E
    heads = range(HEADS_PER_STEP)
    cols = [slice(hh * HEAD_DIM, (hh + 1) * HEAD_DIM) for hh in heads]
    slope2 = [slopes_ref[hg * HEADS_PER_STEP + hh] * LOG2E for hh in heads]
    key_i = lax.broadcasted_iota(jnp.int32, (MOBA_BLOCK, MOBA_BLOCK), 0)
    qry_i = lax.broadcasted_iota(jnp.int32, (MOBA_BLOCK, MOBA_BLOCK), 1)
    rel = qry_i - key_i

    @pl.when(qt == 0)
    def _():
        for j in range(n_blocks):
            rows = slice(j * MOBA_BLOCK, (j + 1) * MOBA_BLOCK)
            vt_ref[:, rows] = vb_ref[0, rows, :].astype(F32).T.astype(BF16)
        for hh in heads:
            bias_ref[hh] = -slope2[hh] * rel.astype(F32)

    own = pl.ds(pl.multiple_of(qt * MOBA_BLOCK, MOBA_BLOCK), MOBA_BLOCK)
    q_ts = [q_ref[0, :, cols[hh]].T for hh in heads]
    for hh in heads:
        qbt_ref[hh] = q_ts[hh].astype(BF16)
    raw = [_dot(kb_ref[0, own, cols[hh]], qbt_ref[hh]) for hh in heads]
    gates = [jnp.dot(kmean_ref[0, :, cols[hh]], q_ts[hh], precision=lax.Precision.HIGHEST,
                     preferred_element_type=F32) for hh in heads]
    stats = []
    for hh in heads:
        blk = lax.broadcasted_iota(jnp.int32, gates[hh].shape, 0)
        gate = jnp.where(blk < qt, gates[hh], NEG_INF)
        _, sel = _top_blocks(gate, blk.astype(F32), qt, axis=0)
        sel_ref[hh] = sel
        t = jnp.where(rel >= 0, raw[hh] * score_scale + bias_ref[hh], NEG_INF)
        m0 = jnp.max(t, axis=0, keepdims=True)
        p = jnp.exp2(t - m0)
        l0 = jnp.sum(p, axis=0, keepdims=True)
        acc_ref[hh] = _dot(vt_ref[cols[hh], own], p.astype(BF16))
        stats.append((m0, l0))

    def past_block(j, stats):
        rows = pl.ds(pl.multiple_of(j * MOBA_BLOCK, MOBA_BLOCK), MOBA_BLOCK)
        blocks_between = ((qt - j) * MOBA_BLOCK).astype(F32)
        raw = [_dot(kb_ref[0, rows, cols[hh]], qbt_ref[hh]) for hh in heads]
        new_stats = []
        for hh in heads:
            m, l = stats[hh]
            t = raw[hh] * score_scale + bias_ref[hh]
            far = slope2[hh] * blocks_between
            picked = sel_ref[hh, pl.ds(j, 1), :] > 0.0
            m_blk = jnp.max(t, axis=0, keepdims=True) - far
            m_new = jnp.where(picked, jnp.maximum(m, m_blk), m)
            shift = jnp.where(picked, m_new + far, jnp.inf)
            p = jnp.exp2(t - shift)
            alpha = jnp.exp2(m - m_new)
            l = alpha * l + jnp.sum(p, axis=0, keepdims=True)
            acc_ref[hh] = alpha * acc_ref[hh] + _dot(vt_ref[cols[hh], rows], p.astype(BF16))
            new_stats.append((m_new, l))
        return tuple(new_stats)

    stats = lax.fori_loop(0, qt, past_block, tuple(stats))
    for hh in heads:
        o_ref[0, :, cols[hh]] = (acc_ref[hh] / stats[hh][1]).T.astype(o_ref.dtype)


def _moba_prompt(q, kb, vb, kmean, slopes):
    bsz, seq, _ = q.shape
    n_qt = seq // MOBA_BLOCK
    width = HEADS_PER_STEP * HEAD_DIM
    return pl.pallas_call(
        _moba_prompt_kernel,
        grid_spec=pltpu.PrefetchScalarGridSpec(
            num_scalar_prefetch=0,
            grid=(bsz, N_HEADS // HEADS_PER_STEP, n_qt),
            in_specs=[pl.BlockSpec(memory_space=pltpu.SMEM),
                      pl.BlockSpec((1, MOBA_BLOCK, width), lambda b, h, t: (b, t, h)),
                      pl.BlockSpec((1, seq, width), lambda b, h, t: (b, 0, h),
                                   pipeline_mode=pl.Buffered(1)),
                      pl.BlockSpec((1, seq, width), lambda b, h, t: (b, 0, h),
                                   pipeline_mode=pl.Buffered(1)),
                      pl.BlockSpec((1, n_qt, width), lambda b, h, t: (b, 0, h))],
            out_specs=pl.BlockSpec((1, MOBA_BLOCK, width), lambda b, h, t: (b, t, h)),
            scratch_shapes=[pltpu.VMEM((width, seq), BF16),
                            pltpu.VMEM((HEADS_PER_STEP, HEAD_DIM, MOBA_BLOCK), BF16),
                            pltpu.VMEM((HEADS_PER_STEP, MOBA_BLOCK, MOBA_BLOCK), F32),
                            pltpu.VMEM((HEADS_PER_STEP, n_qt, MOBA_BLOCK), F32),
                            pltpu.VMEM((HEADS_PER_STEP, HEAD_DIM, MOBA_BLOCK), F32)]),
        out_shape=jax.ShapeDtypeStruct(q.shape, BF16),
        compiler_params=_params(("parallel", "parallel", "arbitrary")),
        name="moba_prompt",
    )(slopes, q, kb, vb, kmean)


MEANS_BLOCKS_PER_STEP = 4


def _cached_means_kernel(pt_ref, *refs, ppb):
    del pt_ref
    o_ref = refs[-1]
    page_refs = refs[:-1]
    for blk in range(MEANS_BLOCKS_PER_STEP):
        total = jnp.sum(page_refs[blk * ppb][0], axis=0)
        for p in range(1, ppb):
            total = total + jnp.sum(page_refs[blk * ppb + p][0], axis=0)
        o_ref[0, blk] = total / MOBA_BLOCK


def _cached_means(cache_k, pt_flat, db, n_pages):
    _, page, n_kv, dh = cache_k.shape
    ppb = MOBA_BLOCK // page
    nfb = (n_pages * page) // MOBA_BLOCK
    pages_per_step = MEANS_BLOCKS_PER_STEP * ppb
    assert nfb % MEANS_BLOCKS_PER_STEP == 0

    def page_spec(p):
        return pl.BlockSpec((1, page, n_kv, dh),
                            lambda b, n, pt: (pt[b * n_pages + n * pages_per_step + p], 0, 0, 0))

    return pl.pallas_call(
        functools.partial(_cached_means_kernel, ppb=ppb),
        grid_spec=pltpu.PrefetchScalarGridSpec(
            num_scalar_prefetch=1,
            grid=(db, nfb // MEANS_BLOCKS_PER_STEP),
            in_specs=[page_spec(p) for p in range(pages_per_step)],
            out_specs=pl.BlockSpec((1, MEANS_BLOCKS_PER_STEP, n_kv, dh),
                                   lambda b, n, pt: (b, n, 0, 0))),
        out_shape=jax.ShapeDtypeStruct((db, nfb, n_kv, dh), F32),
        compiler_params=_params(("parallel", "parallel")),
        name="cached_means",
    )(pt_flat, *([cache_k] * pages_per_step))


def _sample_topk_kernel(q_ref, km_ref, o_ref):
    t = q_ref.shape[1]
    lane = lax.broadcasted_iota(jnp.int32, (t, 128), 1)
    for h in range(N_HEADS):
        cols = slice(h * HEAD_DIM, (h + 1) * HEAD_DIM)
        gate = _dot_nt(q_ref[0, :, cols], km_ref[0, h], precision=lax.Precision.HIGHEST)
        blk_f = lax.broadcasted_iota(jnp.int32, gate.shape, 1).astype(F32)
        picks, _ = _top_blocks(gate, blk_f, None, axis=1)
        out = jnp.zeros((t, 128), F32)
        for r, idx in enumerate(picks):
            out = jnp.where(lane == r, idx, out)
        o_ref[0, h] = out.astype(jnp.int32)


def _sample_topk(q, kmean):
    db, t, _ = q.shape
    nfb = kmean.shape[2]
    assert nfb >= MOBA_TOP_K
    out = pl.pallas_call(
        _sample_topk_kernel,
        grid=(db,),
        in_specs=[pl.BlockSpec((1, t, D_MODEL), lambda b: (b, 0, 0)),
                  pl.BlockSpec((1, N_HEADS, nfb, HEAD_DIM), lambda b: (b, 0, 0, 0))],
        out_specs=pl.BlockSpec((1, N_HEADS, t, 128), lambda b: (b, 0, 0, 0)),
        out_shape=jax.ShapeDtypeStruct((db, N_HEADS, t, 128), jnp.int32),
        compiler_params=_params(("parallel",)),
        name="sample_topk",
    )(q, kmean)
    return out[..., :MOBA_TOP_K]


def _moba_sample_kernel(idx_ref, pt_ref, slopes_ref, q_ref, kn_ref, vn_ref, ck_hbm, cv_hbm,
                        o_ref, kbuf, vbuf, sems, *, n_sel_pages, ppb, n_pages):
    b = pl.program_id(0)
    h = pl.program_id(1)
    n_t = q_ref.shape[1]
    page = kbuf.shape[2]
    past_len = n_pages * page
    step = b * N_HEADS + h
    n_steps = pl.num_programs(0) * N_HEADS
    slot = step % 2
    scale = HEAD_DIM ** -0.5
    slope = slopes_ref[h]

    def logical_page(bb, hh, t, n):
        return idx_ref[((bb * N_HEADS + hh) * n_t + t) * MOBA_TOP_K + n // ppb] * ppb + n % ppb

    def page_copies(bb, hh, sl):
        copies = []
        for t in range(n_t):
            for n in range(n_sel_pages):
                phys = pt_ref[bb * n_pages + logical_page(bb, hh, t, n)]
                dst = t * n_sel_pages + n
                copies.append(pltpu.make_async_copy(ck_hbm.at[phys, :, hh, :], kbuf.at[sl, dst],
                                                    sems.at[0, sl]))
                copies.append(pltpu.make_async_copy(cv_hbm.at[phys, :, hh, :], vbuf.at[sl, dst],
                                                    sems.at[1, sl]))
        return copies

    @pl.when(step == 0)
    def _():
        for cp in page_copies(b, h, slot):
            cp.start()

    @pl.when(step + 1 < n_steps)
    def _():
        nxt = step + 1
        for cp in page_copies(nxt // N_HEADS, nxt % N_HEADS, 1 - slot):
            cp.start()

    for cp in page_copies(b, h, slot):
        cp.wait()

    off = lax.broadcasted_iota(jnp.int32, (page, 1), 0)
    new_i = lax.broadcasted_iota(jnp.int32, (n_t, 1), 0)

    def one_token(t, carry):
        q = q_ref[0, pl.ds(t, 1), :]
        pos_q = past_len + t
        scores = []
        for n in range(n_sel_pages):
            s = jnp.sum(kbuf[slot, t * n_sel_pages + n] * q, axis=1, keepdims=True) * scale
            dist = jnp.abs(pos_q - (logical_page(b, h, t, n) * page + off)).astype(F32)
            scores.append(s - slope * dist)
        s_new = jnp.sum(kn_ref[0] * q, axis=1, keepdims=True) * scale
        d_new = t - new_i
        s_new = jnp.where(d_new >= 0, s_new - slope * d_new.astype(F32), NEG_INF)

        m = jnp.max(s_new, axis=0, keepdims=True)
        for s in scores:
            m = jnp.maximum(m, jnp.max(s, axis=0, keepdims=True))
        p_new = jnp.exp(s_new - m)
        l = jnp.sum(p_new, axis=0, keepdims=True)
        acc = jnp.sum(p_new * vn_ref[0], axis=0, keepdims=True)
        for n, s in enumerate(scores):
            p = jnp.exp(s - m)
            l = l + jnp.sum(p, axis=0, keepdims=True)
            acc = acc + jnp.sum(p * vbuf[slot, t * n_sel_pages + n], axis=0, keepdims=True)
        o_ref[0, pl.ds(t, 1), :] = acc / l
        return carry

    lax.fori_loop(0, n_t, one_token, 0)


def _moba_sample(q, k_new, v_new, cache_k, cache_v, idx_flat, pt_flat, slopes, n_pages):
    db, t, _ = q.shape
    page = cache_k.shape[1]
    ppb = MOBA_BLOCK // page
    n_sel_pages = MOBA_TOP_K * ppb

    def new_spec():
        return pl.BlockSpec((1, t, HEAD_DIM), lambda b, h, idx, pt: (b, 0, h))

    kern = functools.partial(_moba_sample_kernel, n_sel_pages=n_sel_pages, ppb=ppb,
                             n_pages=n_pages)
    return pl.pallas_call(
        kern,
        grid_spec=pltpu.PrefetchScalarGridSpec(
            num_scalar_prefetch=2,
            grid=(db, N_HEADS),
            in_specs=[pl.BlockSpec(memory_space=pltpu.SMEM), new_spec(), new_spec(), new_spec(),
                      pl.BlockSpec(memory_space=pl.ANY), pl.BlockSpec(memory_space=pl.ANY)],
            out_specs=pl.BlockSpec((1, t, HEAD_DIM), lambda b, h, idx, pt: (b, 0, h)),
            scratch_shapes=[pltpu.VMEM((2, t * n_sel_pages, page, HEAD_DIM), F32),
                            pltpu.VMEM((2, t * n_sel_pages, page, HEAD_DIM), F32),
                            pltpu.SemaphoreType.DMA((2, 2))]),
        out_shape=jax.ShapeDtypeStruct(q.shape, F32),
        compiler_params=_params(("arbitrary", "arbitrary")),
        name="moba_sample",
    )(idx_flat, pt_flat, slopes, q, k_new, v_new, cache_k, cache_v)


def kernel(x_prompt, x_sample, cache_k, cache_v, page_table, norm_mix_g, norm_ffn_g,
           gmlp_w_in, gmlp_ln_g, gmlp_ln_b, gmlp_w_s, gmlp_b_s, gmlp_w_out,
           kv_norm_g, w_k, w_v, attn_w_q, attn_w_o, ffn_w1, ffn_w2, final_norm_g):
    bp, sp, d = x_prompt.shape
    bs, ss, _ = x_sample.shape
    depth = norm_mix_g.shape[0]
    n_a = gmlp_w_in.shape[0]
    _, page, n_kv, dh = cache_k.shape
    n_pages = page_table.shape[1]
    assert sp % MOBA_BLOCK == 0 and sp % CHUNK == 0 and ss <= CHUNK
    assert (n_pages * page) % MOBA_BLOCK == 0
    assert n_kv == N_HEADS and dh == HEAD_DIM

    slopes = jnp.exp2(-8.0 * jnp.arange(1, N_HEADS + 1, dtype=F32) / N_HEADS)
    to_bf16 = lambda w: w.astype(BF16)

    rows_p, rows_s = bp * sp, bs * ss
    xp = x_prompt.reshape(rows_p, d)
    xs = x_sample.reshape(rows_s, d)
    pt_flat = page_table.reshape(-1)

    tm_p = 512
    sgu_p, sgu_s = [], []
    for layer in range(depth):
        g = norm_mix_g[layer]
        if layer < n_a:
            w_in, w_out = to_bf16(gmlp_w_in[layer]), to_bf16(gmlp_w_out[layer])
            ws_p = gmlp_w_s[layer]
            bs_p = gmlp_b_s[layer].T
            ws_s = jnp.tile(gmlp_w_s[layer][:, :ss, :ss], (1, bs, bs))
            bs_s = jnp.tile(gmlp_b_s[layer][:, :ss].T, (bs, 1))
            xp, vp = _gmlp(xp, g, w_in, gmlp_ln_g[layer], gmlp_ln_b[layer], ws_p, bs_p,
                           w_out, tm=tm_p, tn=512, tk=512, chunk_len=CHUNK,
                           v_rows=CHUNK, v_period=sp // tm_p)
            xs, vs = _gmlp(xs, g, w_in, gmlp_ln_g[layer], gmlp_ln_b[layer], ws_s, bs_s,
                           w_out, tm=rows_s, tn=512, tk=512, chunk_len=ss,
                           v_rows=rows_s, v_period=1)
            sgu_p.append(vp.reshape(bp, CHUNK, D_GMLP))
            sgu_s.append(vs.reshape(bs, ss, D_GMLP))
        else:
            if layer == n_a:
                w_kv = [to_bf16(w_k), to_bf16(w_v)]
                kp, vp_, kb_p, vb_p, kmean_p = _kv_proj(xp, kv_norm_g, *w_kv, tm=tm_p, tn=512)
                ks_, vs_ = _norm_matmul(xs, kv_norm_g, w_kv, tm=rows_s, tn=512)
                kmean_s = _cached_means(cache_k, pt_flat, bs, n_pages).transpose(0, 2, 1, 3)
            a = layer - n_a
            w_q, w_o = to_bf16(attn_w_q[a]), to_bf16(attn_w_o[a])
            (qp,) = _norm_matmul(xp, g, [w_q], tm=tm_p, tn=512)
            att_p = _moba_prompt(qp.reshape(bp, sp, d), kb_p.reshape(bp, sp, d),
                                 vb_p.reshape(bp, sp, d),
                                 kmean_p.reshape(bp, sp // MOBA_BLOCK, d), slopes)
            xp = _proj_residual(att_p.reshape(rows_p, d), w_o, xp, tm=tm_p, tn=512)

            (qs,) = _norm_matmul(xs, g, [w_q], tm=rows_s, tn=512)
            qs3 = qs.reshape(bs, ss, d)
            idx = _sample_topk(qs3, kmean_s)
            att_s = _moba_sample(qs3, ks_.reshape(bs, ss, d), vs_.reshape(bs, ss, d),
                                 cache_k, cache_v, idx.reshape(-1), pt_flat, slopes, n_pages)
            xs = _proj_residual(att_s.reshape(rows_s, d), w_o, xs, tm=rows_s, tn=512)
        last = layer == depth - 1
        w1, w2 = to_bf16(ffn_w1[layer]), to_bf16(ffn_w2[layer])
        xp = _ffn(xp, norm_ffn_g[layer], w1, w2, final_norm_g, tm=tm_p, tf=512, final_norm=last)
        xs = _ffn(xs, norm_ffn_g[layer], w1, w2, final_norm_g, tm=rows_s, tf=512, final_norm=last)

    kv_shape_p = (bp, sp, n_kv, dh)
    kv_shape_s = (bs, ss, n_kv, dh)
    return (xp.reshape(bp, sp, d), xs.reshape(bs, ss, d),
            kp.reshape(kv_shape_p), vp_.reshape(kv_shape_p),
            ks_.reshape(kv_shape_s), vs_.reshape(kv_shape_s),
            jnp.stack(sgu_p), jnp.stack(sgu_s))
```

```python
import functools

import jax
import jax.numpy as jnp
from jax import lax
from jax.experimental import pallas as pl
from jax.experimental.pallas import tpu as pltpu

D_MODEL = 2048
N_HEADS = 16
HEAD_DIM = 128
MOBA_BLOCK = 256
MOBA_TOP_K = 3
CHUNK = 128
D_GMLP = 2 * D_MODEL
N_SGU_GROUPS = 16
SGU_GROUP = D_GMLP // N_SGU_GROUPS
D_FF = 4 * D_MODEL
NORM_EPS = 1e-6

V7X_VMEM_LIMIT_BYTES = 56 * 1024 * 1024

F32 = jnp.float32
BF16 = jnp.bfloat16
NEG_INF = float("-inf")
LOG2E = 1.4426950408889634


def _params(semantics):
    return pltpu.CompilerParams(dimension_semantics=semantics,
                                vmem_limit_bytes=V7X_VMEM_LIMIT_BYTES)


def _rmsnorm_rows(x, g):
    return x * lax.rsqrt(jnp.mean(x * x, axis=-1, keepdims=True) + NORM_EPS) * g


def _dot(a, b):
    return jnp.dot(a, b, preferred_element_type=F32)


def _dot_nt(a, b, precision=None):
    return lax.dot_general(a, b, (((1,), (1,)), ((), ())), precision=precision,
                           preferred_element_type=F32)


def _weight_spec(w, layer, block, index_map):
    if w.ndim == 2:
        return pl.BlockSpec(block, index_map)
    return pl.BlockSpec((None,) + block, lambda *args: (layer,) + tuple(index_map(*args)))


def _use_weight(w_ref, wb_ref):
    w = w_ref[...].astype(BF16)
    if wb_ref is not None:
        wb_ref[...] = w
    return w


def _norm_matmul_kernel(*refs, n_w, emit_bf16):
    x_ref, g_ref = refs[:2]
    w_refs = refs[2:2 + n_w]
    o_refs = refs[2 + n_w:2 + 2 * n_w]
    wb_refs = refs[2 + 2 * n_w:2 + 3 * n_w] if emit_bf16 else [None] * n_w
    xn_ref = refs[-1]

    @pl.when(pl.program_id(1) == 0)
    def _():
        xn_ref[...] = _rmsnorm_rows(x_ref[...], g_ref[...]).astype(BF16)

    xn = xn_ref[...]
    for w_ref, o_ref, wb_ref in zip(w_refs, o_refs, wb_refs):
        o_ref[...] = _dot(xn, _use_weight(w_ref, wb_ref))


def _norm_matmul(x, g, ws, *, tm, tn, layer=0, emit_bf16=False):
    rows, d = x.shape
    n = ws[0].shape[-1]
    n_w = len(ws)
    out_specs = [pl.BlockSpec((tm, tn), lambda i, j: (i, j)) for _ in ws]
    out_shape = [jax.ShapeDtypeStruct((rows, n), F32) for _ in ws]
    if emit_bf16:
        assert rows == tm
        out_specs += [pl.BlockSpec((d, tn), lambda i, j: (0, j)) for _ in ws]
        out_shape += [jax.ShapeDtypeStruct((d, n), BF16) for _ in ws]
    return pl.pallas_call(
        functools.partial(_norm_matmul_kernel, n_w=n_w, emit_bf16=emit_bf16),
        grid=(rows // tm, n // tn),
        in_specs=[pl.BlockSpec((tm, d), lambda i, j: (i, 0)),
                  pl.BlockSpec((1, d), lambda i, j: (0, 0))]
                 + [_weight_spec(w, layer, (d, tn), lambda i, j: (0, j)) for w in ws],
        out_specs=out_specs,
        out_shape=out_shape,
        scratch_shapes=[pltpu.VMEM((tm, d), BF16)],
        compiler_params=_params(("parallel", "arbitrary")),
        name="norm_matmul",
    )(x, g.reshape(1, d), *ws)


def _proj_residual_kernel(a_ref, w_ref, x_ref, o_ref, wb_ref=None):
    o_ref[...] = x_ref[...] + _dot(a_ref[...].astype(BF16), _use_weight(w_ref, wb_ref))


def _proj_residual(a, w, x, *, tm, tn, layer=0, emit_bf16=False):
    rows, k = a.shape
    n = w.shape[-1]
    out_specs = [pl.BlockSpec((tm, tn), lambda i, j: (i, j))]
    out_shape = [jax.ShapeDtypeStruct((rows, n), F32)]
    if emit_bf16:
        assert rows == tm
        out_specs.append(pl.BlockSpec((k, tn), lambda i, j: (0, j)))
        out_shape.append(jax.ShapeDtypeStruct((k, n), BF16))
    return pl.pallas_call(
        _proj_residual_kernel,
        grid=(rows // tm, n // tn),
        in_specs=[pl.BlockSpec((tm, k), lambda i, j: (i, 0)),
                  _weight_spec(w, layer, (k, tn), lambda i, j: (0, j)),
                  pl.BlockSpec((tm, tn), lambda i, j: (i, j))],
        out_specs=out_specs,
        out_shape=out_shape,
        compiler_params=_params(("parallel", "parallel")),
        name="proj_residual",
    )(a, w, x)


def _ffn_kernel(x_ref, g_ref, w1_ref, w2_ref, gf_ref, o_ref, *rest, final_norm, emit_bf16):
    w1b_ref, w2b_ref = rest[:2] if emit_bf16 else (None, None)
    xn_ref = rest[-1]
    j = pl.program_id(1)

    @pl.when(j == 0)
    def _():
        x = x_ref[...]
        xn_ref[...] = _rmsnorm_rows(x, g_ref[...]).astype(BF16)
        o_ref[...] = x

    h = jnp.maximum(_dot(xn_ref[...], _use_weight(w1_ref, w1b_ref)), 0.0)
    o_ref[...] += _dot((h * h).astype(BF16), _use_weight(w2_ref, w2b_ref))

    if final_norm:
        @pl.when(j == pl.num_programs(1) - 1)
        def _():
            o_ref[...] = _rmsnorm_rows(o_ref[...], gf_ref[...])


def _ffn(x, g, w1, w2, gf, *, tm, tf, final_norm, layer=0, emit_bf16=False):
    rows, d = x.shape
    dff = w1.shape[-1]
    out_specs = [pl.BlockSpec((tm, d), lambda i, j: (i, 0))]
    out_shape = [jax.ShapeDtypeStruct((rows, d), F32)]
    if emit_bf16:
        assert rows == tm
        out_specs += [pl.BlockSpec((d, tf), lambda i, j: (0, j)),
                      pl.BlockSpec((tf, d), lambda i, j: (j, 0))]
        out_shape += [jax.ShapeDtypeStruct((d, dff), BF16), jax.ShapeDtypeStruct((dff, d), BF16)]
    return pl.pallas_call(
        functools.partial(_ffn_kernel, final_norm=final_norm, emit_bf16=emit_bf16),
        grid=(rows // tm, dff // tf),
        in_specs=[pl.BlockSpec((tm, d), lambda i, j: (i, 0)),
                  pl.BlockSpec((1, d), lambda i, j: (0, 0)),
                  _weight_spec(w1, layer, (d, tf), lambda i, j: (0, j)),
                  _weight_spec(w2, layer, (tf, d), lambda i, j: (j, 0)),
                  pl.BlockSpec((1, d), lambda i, j: (0, 0))],
        out_specs=out_specs,
        out_shape=out_shape,
        scratch_shapes=[pltpu.VMEM((tm, d), BF16)],
        compiler_params=_params(("parallel", "arbitrary")),
        name="ffn",
    )(x, g.reshape(1, d), w1, w2, gf.reshape(1, d))


def _gmlp_kernel(x_ref, g_ref, win_ref, lng_ref, lnb_ref, ws_ref, bs_ref, wout_ref,
                 y_ref, vout_ref, *rest, nv, tn, tk, chunk_len, mix_rows, emit_bf16):
    winb_ref, woutb_ref = rest[:2] if emit_bf16 else (None, None)
    xn_ref, v_ref, h_ref, wsb_ref = rest[-4:]
    s = pl.program_id(1)
    tm = x_ref.shape[0]

    @pl.when(s == 0)
    def _():
        x = x_ref[...]
        xn_ref[...] = _rmsnorm_rows(x, g_ref[...]).astype(BF16)
        y_ref[...] = x

    @pl.when(s < nv)
    def _():
        z = jax.nn.gelu(_dot(xn_ref[...], _use_weight(win_ref, winb_ref)))
        v_ref[:, pl.ds(pl.multiple_of(s * tn, tn), tn)] = z

    @pl.when(s == nv)
    def _():
        r = lax.broadcasted_iota(jnp.int32, (mix_rows, mix_rows), 0)
        c = lax.broadcasted_iota(jnp.int32, (mix_rows, mix_rows), 1)
        keep = (r // chunk_len == c // chunk_len) & (c <= r)
        for grp in range(N_SGU_GROUPS):
            wsb_ref[grp] = jnp.where(keep, ws_ref[grp], 0.0).astype(BF16)
        ln_g = lng_ref[...]
        ln_b = lnb_ref[...]

        def mix_tile(t, carry):
            rows = pl.ds(pl.multiple_of(t * mix_rows, mix_rows), mix_rows)
            v = v_ref[rows, :]
            mu = jnp.mean(v, axis=-1, keepdims=True)
            vc = v - mu
            var = jnp.mean(vc * vc, axis=-1, keepdims=True)
            vn = vc * lax.rsqrt(var + NORM_EPS) * ln_g + ln_b
            vout_ref[...] = vn
            for grp in range(N_SGU_GROUPS):
                cols = slice(grp * SGU_GROUP, (grp + 1) * SGU_GROUP)
                v_ref[rows, cols] = (_dot(wsb_ref[grp], vn[:, cols].astype(BF16))
                                     + bs_ref[:, grp:grp + 1])
            return carry

        lax.fori_loop(0, tm // mix_rows, mix_tile, 0)

    @pl.when((s >= nv) & (s < 2 * nv))
    def _():
        cols = pl.ds(pl.multiple_of((s - nv) * tn, tn), tn)
        u = jax.nn.gelu(_dot(xn_ref[...], _use_weight(win_ref, winb_ref)))
        h_ref[:, cols] = (u * v_ref[:, cols]).astype(BF16)

    @pl.when(s >= 2 * nv)
    def _():
        k = s - 2 * nv
        h = h_ref[:, pl.ds(pl.multiple_of(k * tk, tk), tk)]
        y_ref[...] += _dot(h, _use_weight(wout_ref, woutb_ref))


def _gmlp(x, g, w_in, ln_g, ln_b, ws_tiled, bs_tiled, w_out, *, tm, tn, tk, chunk_len,
          v_period, layer=0, emit_bf16=False):
    rows, d = x.shape
    mix_rows = ws_tiled.shape[1]
    nv = D_GMLP // tn
    n_out = D_GMLP // tk
    n_vblocks = rows // (tm * v_period)
    win_tile = lambda i, s: (0, jnp.where(s < nv, s + nv, jnp.minimum(s - nv, nv - 1)))
    wout_tile = lambda i, s: (jnp.maximum(s - 2 * nv, 0), 0)
    out_specs = [pl.BlockSpec((tm, d), lambda i, s: (i, 0)),
                 pl.BlockSpec((mix_rows, D_GMLP), lambda i, s: (i // v_period, 0))]
    out_shape = [jax.ShapeDtypeStruct((rows, d), F32),
                 jax.ShapeDtypeStruct((n_vblocks * mix_rows, D_GMLP), F32)]
    if emit_bf16:
        assert rows == tm
        out_specs += [pl.BlockSpec((d, tn), win_tile), pl.BlockSpec((tk, d), wout_tile)]
        out_shape += [jax.ShapeDtypeStruct((d, 2 * D_GMLP), BF16),
                      jax.ShapeDtypeStruct((D_GMLP, d), BF16)]
    kern = functools.partial(_gmlp_kernel, nv=nv, tn=tn, tk=tk, chunk_len=chunk_len,
                             mix_rows=mix_rows, emit_bf16=emit_bf16)
    return pl.pallas_call(
        kern,
        grid=(rows // tm, 2 * nv + n_out),
        in_specs=[pl.BlockSpec((tm, d), lambda i, s: (i, 0)),
                  pl.BlockSpec((1, d), lambda i, s: (0, 0)),
                  _weight_spec(w_in, layer, (d, tn), win_tile),
                  pl.BlockSpec((1, D_GMLP), lambda i, s: (0, 0)),
                  pl.BlockSpec((1, D_GMLP), lambda i, s: (0, 0)),
                  pl.BlockSpec((N_SGU_GROUPS, mix_rows, mix_rows), lambda i, s: (0, 0, 0)),
                  pl.BlockSpec((mix_rows, N_SGU_GROUPS), lambda i, s: (0, 0)),
                  _weight_spec(w_out, layer, (tk, d), wout_tile)],
        out_specs=out_specs,
        out_shape=out_shape,
        scratch_shapes=[pltpu.VMEM((tm, d), BF16),
                        pltpu.VMEM((tm, D_GMLP), F32),
                        pltpu.VMEM((tm, D_GMLP), BF16),
                        pltpu.VMEM((N_SGU_GROUPS, mix_rows, mix_rows), BF16)],
        compiler_params=_params(("arbitrary", "arbitrary")),
        name="gmlp",
    )(x, g.reshape(1, d), w_in, ln_g.reshape(1, D_GMLP), ln_b.reshape(1, D_GMLP),
      ws_tiled, bs_tiled, w_out)


def _top_blocks(gate, blk, n_take, axis):
    nb = gate.shape[axis]
    picks = []
    sel = jnp.zeros_like(gate)
    for r in range(MOBA_TOP_K):
        m = jnp.max(gate, axis=axis, keepdims=True)
        idx = jnp.min(jnp.where(gate == m, blk, float(nb)), axis=axis, keepdims=True)
        hit = blk == idx
        picks.append(idx)
        counts = 1.0 if n_take is None else jnp.where(r < n_take, 1.0, 0.0)
        sel = jnp.maximum(sel, jnp.where(hit, counts, 0.0))
        gate = jnp.where(hit, NEG_INF, gate)
    return picks, sel


def _kv_proj_kernel(x_ref, g_ref, wk_ref, wv_ref, k_ref, v_ref, kb_ref, vb_ref, kmean_ref, xn_ref):
    @pl.when(pl.program_id(1) == 0)
    def _():
        xn_ref[...] = _rmsnorm_rows(x_ref[...], g_ref[...]).astype(BF16)

    xn = xn_ref[...]
    k = _dot(xn, wk_ref[...])
    v = _dot(xn, wv_ref[...])
    k_ref[...] = k
    v_ref[...] = v
    kb_ref[...] = k.astype(BF16)
    vb_ref[...] = v.astype(BF16)
    for blk in range(kmean_ref.shape[0]):
        rows = slice(blk * MOBA_BLOCK, (blk + 1) * MOBA_BLOCK)
        kmean_ref[blk] = jnp.mean(k[rows, :], axis=0, keepdims=True)


def _kv_proj(x, g, wk, wv, *, tm, tn):
    rows, d = x.shape
    n = wk.shape[1]
    blocks_per_tile = tm // MOBA_BLOCK
    tile = lambda: pl.BlockSpec((tm, tn), lambda i, j: (i, j))
    return pl.pallas_call(
        _kv_proj_kernel,
        grid=(rows // tm, n // tn),
        in_specs=[pl.BlockSpec((tm, d), lambda i, j: (i, 0)),
                  pl.BlockSpec((1, d), lambda i, j: (0, 0)),
                  pl.BlockSpec((d, tn), lambda i, j: (0, j)),
                  pl.BlockSpec((d, tn), lambda i, j: (0, j))],
        out_specs=[tile(), tile(), tile(), tile(),
                   pl.BlockSpec((blocks_per_tile, 1, tn), lambda i, j: (i, 0, j))],
        out_shape=[jax.ShapeDtypeStruct((rows, n), F32), jax.ShapeDtypeStruct((rows, n), F32),
                   jax.ShapeDtypeStruct((rows, n), BF16), jax.ShapeDtypeStruct((rows, n), BF16),
                   jax.ShapeDtypeStruct((rows // MOBA_BLOCK, 1, n), F32)],
        scratch_shapes=[pltpu.VMEM((tm, d), BF16)],
        compiler_params=_params(("parallel", "arbitrary")),
        name="kv_proj",
    )(x, g.reshape(1, d), wk, wv)


HEADS_PER_STEP = 8


def _moba_prompt_kernel(slopes_ref, q_ref, kb_ref, vb_ref, kmean_ref, o_ref,
                        vt_ref, qaug_ref, kaug_ref, sel_ref, acc_ref):
    hg = pl.program_id(1)
    qt = pl.program_id(2)
    n_blocks = kb_ref.shape[1] // MOBA_BLOCK
    score_scale = (HEAD_DIM ** -0.5) * LOG2E
    heads = range(HEADS_PER_STEP)
    cols = [slice(hh * HEAD_DIM, (hh + 1) * HEAD_DIM) for hh in heads]
    slope2 = [slopes_ref[hg * HEADS_PER_STEP + hh] * LOG2E for hh in heads]

    @pl.when(qt == 0)
    def _():
        for j in range(n_blocks):
            rows = slice(j * MOBA_BLOCK, (j + 1) * MOBA_BLOCK)
            vt_ref[:, rows] = vb_ref[0, rows, :].astype(F32).T.astype(BF16)
        key_i = lax.broadcasted_iota(jnp.int32, (MOBA_BLOCK, HEAD_DIM), 0)
        piece = lax.broadcasted_iota(jnp.int32, (MOBA_BLOCK, HEAD_DIM), 1)
        kaug_ref[...] = jnp.where(piece < 3, key_i, 0).astype(BF16)
        part = lax.broadcasted_iota(jnp.int32, (HEAD_DIM, MOBA_BLOCK), 0)
        for hh in heads:
            whole = jnp.full((HEAD_DIM, MOBA_BLOCK), slope2[hh], F32)
            hi = whole.astype(BF16).astype(F32)
            mid = (whole - hi).astype(BF16).astype(F32)
            lo = whole - hi - mid
            pieces = jnp.where(part == 0, hi, jnp.where(part == 1, mid, jnp.where(part == 2, lo, 0.0)))
            qaug_ref[hh, HEAD_DIM:, :] = pieces.astype(BF16)

    key_i = lax.broadcasted_iota(jnp.int32, (MOBA_BLOCK, MOBA_BLOCK), 0)
    qry_i = lax.broadcasted_iota(jnp.int32, (MOBA_BLOCK, MOBA_BLOCK), 1)
    causal = qry_i >= key_i
    own = pl.ds(pl.multiple_of(qt * MOBA_BLOCK, MOBA_BLOCK), MOBA_BLOCK)
    q_ts = [q_ref[0, :, cols[hh]].T for hh in heads]
    for hh in heads:
        qaug_ref[hh, :HEAD_DIM, :] = (q_ts[hh] * score_scale).astype(BF16)
    kaug = kaug_ref[...]

    def scores(rows, hh):
        keys = jnp.concatenate([kb_ref[0, rows, cols[hh]], kaug], axis=1)
        return _dot(keys, qaug_ref[hh])

    raw = [scores(own, hh) for hh in heads]
    gates = [jnp.dot(kmean_ref[0, :, cols[hh]], q_ts[hh], precision=lax.Precision.HIGHEST,
                     preferred_element_type=F32) for hh in heads]
    stats = []
    for hh in heads:
        blk = lax.broadcasted_iota(jnp.int32, gates[hh].shape, 0)
        gate = jnp.where(blk < qt, gates[hh], NEG_INF)
        _, sel = _top_blocks(gate, blk.astype(F32), qt, axis=0)
        sel_ref[hh] = sel
        t = jnp.where(causal, raw[hh], NEG_INF)
        m0 = jnp.max(t, axis=0, keepdims=True)
        p = jnp.exp2(t - m0)
        l0 = jnp.sum(p, axis=0, keepdims=True)
        acc_ref[hh] = _dot(vt_ref[cols[hh], own], p.astype(BF16))
        stats.append((m0, l0))

    def past_block(j, stats):
        rows = pl.ds(pl.multiple_of(j * MOBA_BLOCK, MOBA_BLOCK), MOBA_BLOCK)
        blocks_between = ((qt - j) * MOBA_BLOCK).astype(F32)
        raw = [scores(rows, hh) for hh in heads]
        new_stats = []
        for hh in heads:
            m, l = stats[hh]
            t = raw[hh]
            far = slope2[hh] * blocks_between
            picked = sel_ref[hh, pl.ds(j, 1), :] > 0.0
            m_blk = jnp.max(t, axis=0, keepdims=True) - far
            m_new = jnp.where(picked, jnp.maximum(m, m_blk), m)
            shift = jnp.where(picked, m_new + far, jnp.inf)
            p = jnp.exp2(t - shift)
            alpha = jnp.exp2(m - m_new)
            l = alpha * l + jnp.sum(p, axis=0, keepdims=True)
            acc_ref[hh] = alpha * acc_ref[hh] + _dot(vt_ref[cols[hh], rows], p.astype(BF16))
            new_stats.append((m_new, l))
        return tuple(new_stats)

    stats = lax.fori_loop(0, qt, past_block, tuple(stats))
    for hh in heads:
        o_ref[0, :, cols[hh]] = (acc_ref[hh] / stats[hh][1]).T.astype(o_ref.dtype)


def _moba_prompt(q, kb, vb, kmean, slopes):
    bsz, seq, _ = q.shape
    n_qt = seq // MOBA_BLOCK
    width = HEADS_PER_STEP * HEAD_DIM
    return pl.pallas_call(
        _moba_prompt_kernel,
        grid_spec=pltpu.PrefetchScalarGridSpec(
            num_scalar_prefetch=0,
            grid=(bsz, N_HEADS // HEADS_PER_STEP, n_qt),
            in_specs=[pl.BlockSpec(memory_space=pltpu.SMEM),
                      pl.BlockSpec((1, MOBA_BLOCK, width), lambda b, h, t: (b, t, h)),
                      pl.BlockSpec((1, seq, width), lambda b, h, t: (b, 0, h),
                                   pipeline_mode=pl.Buffered(1)),
                      pl.BlockSpec((1, seq, width), lambda b, h, t: (b, 0, h),
                                   pipeline_mode=pl.Buffered(1)),
                      pl.BlockSpec((1, n_qt, width), lambda b, h, t: (b, 0, h))],
            out_specs=pl.BlockSpec((1, MOBA_BLOCK, width), lambda b, h, t: (b, t, h)),
            scratch_shapes=[pltpu.VMEM((width, seq), BF16),
                            pltpu.VMEM((HEADS_PER_STEP, 2 * HEAD_DIM, MOBA_BLOCK), BF16),
                            pltpu.VMEM((MOBA_BLOCK, HEAD_DIM), BF16),
                            pltpu.VMEM((HEADS_PER_STEP, n_qt, MOBA_BLOCK), F32),
                            pltpu.VMEM((HEADS_PER_STEP, HEAD_DIM, MOBA_BLOCK), F32)]),
        out_shape=jax.ShapeDtypeStruct(q.shape, BF16),
        compiler_params=_params(("parallel", "parallel", "arbitrary")),
        name="moba_prompt",
    )(slopes, q, kb, vb, kmean)


MEANS_BLOCKS_PER_STEP = 4


def _cached_means_kernel(pt_ref, *refs, ppb):
    del pt_ref
    o_ref = refs[-1]
    page_refs = refs[:-1]
    for blk in range(MEANS_BLOCKS_PER_STEP):
        total = jnp.sum(page_refs[blk * ppb][0], axis=0)
        for p in range(1, ppb):
            total = total + jnp.sum(page_refs[blk * ppb + p][0], axis=0)
        o_ref[0, blk] = total / MOBA_BLOCK


def _cached_means(cache_k, pt_flat, db, n_pages):
    _, page, n_kv, dh = cache_k.shape
    ppb = MOBA_BLOCK // page
    nfb = (n_pages * page) // MOBA_BLOCK
    pages_per_step = MEANS_BLOCKS_PER_STEP * ppb
    assert nfb % MEANS_BLOCKS_PER_STEP == 0

    def page_spec(p):
        return pl.BlockSpec((1, page, n_kv, dh),
                            lambda b, n, pt: (pt[b * n_pages + n * pages_per_step + p], 0, 0, 0))

    return pl.pallas_call(
        functools.partial(_cached_means_kernel, ppb=ppb),
        grid_spec=pltpu.PrefetchScalarGridSpec(
            num_scalar_prefetch=1,
            grid=(db, nfb // MEANS_BLOCKS_PER_STEP),
            in_specs=[page_spec(p) for p in range(pages_per_step)],
            out_specs=pl.BlockSpec((1, MEANS_BLOCKS_PER_STEP, n_kv, dh),
                                   lambda b, n, pt: (b, n, 0, 0))),
        out_shape=jax.ShapeDtypeStruct((db, nfb, n_kv, dh), F32),
        compiler_params=_params(("parallel", "parallel")),
        name="cached_means",
    )(pt_flat, *([cache_k] * pages_per_step))


def _sample_topk_kernel(q_ref, km_ref, o_ref):
    t = q_ref.shape[1]
    lane = lax.broadcasted_iota(jnp.int32, (t, 128), 1)
    for h in range(N_HEADS):
        cols = slice(h * HEAD_DIM, (h + 1) * HEAD_DIM)
        gate = _dot_nt(q_ref[0, :, cols], km_ref[0, h], precision=lax.Precision.HIGHEST)
        blk_f = lax.broadcasted_iota(jnp.int32, gate.shape, 1).astype(F32)
        picks, _ = _top_blocks(gate, blk_f, None, axis=1)
        out = jnp.zeros((t, 128), F32)
        for r, idx in enumerate(picks):
            out = jnp.where(lane == r, idx, out)
        o_ref[0, h] = out.astype(jnp.int32)


def _sample_topk(q, kmean):
    db, t, _ = q.shape
    nfb = kmean.shape[2]
    assert nfb >= MOBA_TOP_K
    out = pl.pallas_call(
        _sample_topk_kernel,
        grid=(db,),
        in_specs=[pl.BlockSpec((1, t, D_MODEL), lambda b: (b, 0, 0)),
                  pl.BlockSpec((1, N_HEADS, nfb, HEAD_DIM), lambda b: (b, 0, 0, 0))],
        out_specs=pl.BlockSpec((1, N_HEADS, t, 128), lambda b: (b, 0, 0, 0)),
        out_shape=jax.ShapeDtypeStruct((db, N_HEADS, t, 128), jnp.int32),
        compiler_params=_params(("parallel",)),
        name="sample_topk",
    )(q, kmean)
    return out[..., :MOBA_TOP_K]


def _moba_sample_kernel(idx_ref, pt_ref, slopes_ref, q_ref, kn_ref, vn_ref, ck_hbm, cv_hbm,
                        o_ref, kbuf, vbuf, sems, *, n_sel_pages, ppb, n_pages):
    b = pl.program_id(0)
    h = pl.program_id(1)
    n_t = q_ref.shape[1]
    page = kbuf.shape[2]
    past_len = n_pages * page
    step = b * N_HEADS + h
    n_steps = pl.num_programs(0) * N_HEADS
    slot = step % 2
    scale = HEAD_DIM ** -0.5
    slope = slopes_ref[h]

    def logical_page(bb, hh, t, n):
        return idx_ref[((bb * N_HEADS + hh) * n_t + t) * MOBA_TOP_K + n // ppb] * ppb + n % ppb

    def page_copies(bb, hh, sl):
        copies = []
        for t in range(n_t):
            for n in range(n_sel_pages):
                phys = pt_ref[bb * n_pages + logical_page(bb, hh, t, n)]
                dst = t * n_sel_pages + n
                copies.append(pltpu.make_async_copy(ck_hbm.at[phys, :, hh, :], kbuf.at[sl, dst],
                                                    sems.at[0, sl]))
                copies.append(pltpu.make_async_copy(cv_hbm.at[phys, :, hh, :], vbuf.at[sl, dst],
                                                    sems.at[1, sl]))
        return copies

    @pl.when(step == 0)
    def _():
        for cp in page_copies(b, h, slot):
            cp.start()

    @pl.when(step + 1 < n_steps)
    def _():
        nxt = step + 1
        for cp in page_copies(nxt // N_HEADS, nxt % N_HEADS, 1 - slot):
            cp.start()

    for cp in page_copies(b, h, slot):
        cp.wait()

    off = lax.broadcasted_iota(jnp.int32, (page, 1), 0)
    new_i = lax.broadcasted_iota(jnp.int32, (n_t, 1), 0)

    def one_token(t, carry):
        q = q_ref[0, pl.ds(t, 1), :]
        pos_q = past_len + t
        scores = []
        for n in range(n_sel_pages):
            s = jnp.sum(kbuf[slot, t * n_sel_pages + n] * q, axis=1, keepdims=True) * scale
            dist = jnp.abs(pos_q - (logical_page(b, h, t, n) * page + off)).astype(F32)
            scores.append(s - slope * dist)
        s_new = jnp.sum(kn_ref[0] * q, axis=1, keepdims=True) * scale
        d_new = t - new_i
        s_new = jnp.where(d_new >= 0, s_new - slope * d_new.astype(F32), NEG_INF)

        m = jnp.max(s_new, axis=0, keepdims=True)
        for s in scores:
            m = jnp.maximum(m, jnp.max(s, axis=0, keepdims=True))
        p_new = jnp.exp(s_new - m)
        l = jnp.sum(p_new, axis=0, keepdims=True)
        acc = jnp.sum(p_new * vn_ref[0], axis=0, keepdims=True)
        for n, s in enumerate(scores):
            p = jnp.exp(s - m)
            l = l + jnp.sum(p, axis=0, keepdims=True)
            acc = acc + jnp.sum(p * vbuf[slot, t * n_sel_pages + n], axis=0, keepdims=True)
        o_ref[0, pl.ds(t, 1), :] = acc / l
        return carry

    lax.fori_loop(0, n_t, one_token, 0)


def _moba_sample(q, k_new, v_new, cache_k, cache_v, idx_flat, pt_flat, slopes, n_pages):
    db, t, _ = q.shape
    page = cache_k.shape[1]
    ppb = MOBA_BLOCK // page
    n_sel_pages = MOBA_TOP_K * ppb

    def new_spec():
        return pl.BlockSpec((1, t, HEAD_DIM), lambda b, h, idx, pt: (b, 0, h))

    kern = functools.partial(_moba_sample_kernel, n_sel_pages=n_sel_pages, ppb=ppb,
                             n_pages=n_pages)
    return pl.pallas_call(
        kern,
        grid_spec=pltpu.PrefetchScalarGridSpec(
            num_scalar_prefetch=2,
            grid=(db, N_HEADS),
            in_specs=[pl.BlockSpec(memory_space=pltpu.SMEM), new_spec(), new_spec(), new_spec(),
                      pl.BlockSpec(memory_space=pl.ANY), pl.BlockSpec(memory_space=pl.ANY)],
            out_specs=pl.BlockSpec((1, t, HEAD_DIM), lambda b, h, idx, pt: (b, 0, h)),
            scratch_shapes=[pltpu.VMEM((2, t * n_sel_pages, page, HEAD_DIM), F32),
                            pltpu.VMEM((2, t * n_sel_pages, page, HEAD_DIM), F32),
                            pltpu.SemaphoreType.DMA((2, 2))]),
        out_shape=jax.ShapeDtypeStruct(q.shape, F32),
        compiler_params=_params(("arbitrary", "arbitrary")),
        name="moba_sample",
    )(idx_flat, pt_flat, slopes, q, k_new, v_new, cache_k, cache_v)


def kernel(x_prompt, x_sample, cache_k, cache_v, page_table, norm_mix_g, norm_ffn_g,
           gmlp_w_in, gmlp_ln_g, gmlp_ln_b, gmlp_w_s, gmlp_b_s, gmlp_w_out,
           kv_norm_g, w_k, w_v, attn_w_q, attn_w_o, ffn_w1, ffn_w2, final_norm_g):
    bp, sp, d = x_prompt.shape
    bs, ss, _ = x_sample.shape
    depth = norm_mix_g.shape[0]
    n_a = gmlp_w_in.shape[0]
    _, page, n_kv, dh = cache_k.shape
    n_pages = page_table.shape[1]
    assert sp % MOBA_BLOCK == 0 and sp % CHUNK == 0 and ss <= CHUNK
    assert (n_pages * page) % MOBA_BLOCK == 0
    assert n_kv == N_HEADS and dh == HEAD_DIM

    slopes = jnp.exp2(-8.0 * jnp.arange(1, N_HEADS + 1, dtype=F32) / N_HEADS)

    rows_p, rows_s = bp * sp, bs * ss
    xp = x_prompt.reshape(rows_p, d)
    xs = x_sample.reshape(rows_s, d)
    pt_flat = page_table.reshape(-1)

    tm_p = 512
    sgu_p, sgu_s = [], []
    for layer in range(depth):
        g = norm_mix_g[layer]
        if layer < n_a:
            ws_p = gmlp_w_s[layer]
            bs_p = gmlp_b_s[layer].T
            ws_s = jnp.tile(gmlp_w_s[layer][:, :ss, :ss], (1, bs, bs))
            bs_s = jnp.tile(gmlp_b_s[layer][:, :ss].T, (bs, 1))
            xs, vs, w_in, w_out = _gmlp(
                xs, g, gmlp_w_in, gmlp_ln_g[layer], gmlp_ln_b[layer], ws_s, bs_s, gmlp_w_out,
                tm=rows_s, tn=512, tk=512, chunk_len=ss, v_period=1, layer=layer, emit_bf16=True)
            xp, vp = _gmlp(xp, g, w_in, gmlp_ln_g[layer], gmlp_ln_b[layer], ws_p, bs_p, w_out,
                           tm=tm_p, tn=1024, tk=512, chunk_len=CHUNK, v_period=sp // tm_p)
            sgu_p.append(vp.reshape(bp, CHUNK, D_GMLP))
            sgu_s.append(vs.reshape(bs, ss, D_GMLP))
        else:
            if layer == n_a:
                ks_, vs_, wk_b, wv_b = _norm_matmul(xs, kv_norm_g, [w_k, w_v], tm=rows_s, tn=512,
                                                   emit_bf16=True)
                kp, vp_, kb_p, vb_p, kmean_p = _kv_proj(xp, kv_norm_g, wk_b, wv_b, tm=tm_p, tn=512)
                kmean_s = _cached_means(cache_k, pt_flat, bs, n_pages).transpose(0, 2, 1, 3)
            a = layer - n_a
            qs, wq_b = _norm_matmul(xs, g, [attn_w_q], tm=rows_s, tn=512, layer=a, emit_bf16=True)
            qs3 = qs.reshape(bs, ss, d)
            idx = _sample_topk(qs3, kmean_s)
            att_s = _moba_sample(qs3, ks_.reshape(bs, ss, d), vs_.reshape(bs, ss, d),
                                 cache_k, cache_v, idx.reshape(-1), pt_flat, slopes, n_pages)
            xs, wo_b = _proj_residual(att_s.reshape(rows_s, d), attn_w_o, xs, tm=rows_s, tn=512,
                                      layer=a, emit_bf16=True)

            (qp,) = _norm_matmul(xp, g, [wq_b], tm=tm_p, tn=512)
            att_p = _moba_prompt(qp.reshape(bp, sp, d), kb_p.reshape(bp, sp, d),
                                 vb_p.reshape(bp, sp, d),
                                 kmean_p.reshape(bp, sp // MOBA_BLOCK, d), slopes)
            (xp,) = _proj_residual(att_p.reshape(rows_p, d), wo_b, xp, tm=tm_p, tn=512)
        last = layer == depth - 1
        xs, w1_b, w2_b = _ffn(xs, norm_ffn_g[layer], ffn_w1, ffn_w2, final_norm_g, tm=rows_s,
                              tf=512, final_norm=last, layer=layer, emit_bf16=True)
        (xp,) = _ffn(xp, norm_ffn_g[layer], w1_b, w2_b, final_norm_g, tm=tm_p, tf=1024,
                     final_norm=last)

    kv_shape_p = (bp, sp, n_kv, dh)
    kv_shape_s = (bs, ss, n_kv, dh)
    return (xp.reshape(bp, sp, d), xs.reshape(bs, ss, d),
            kp.reshape(kv_shape_p), vp_.reshape(kv_shape_p),
            ks_.reshape(kv_shape_s), vs_.reshape(kv_shape_s),
            jnp.stack(sgu_p), jnp.stack(sgu_s))
```

```python
import functools

import jax
import jax.numpy as jnp
from jax import lax
from jax.experimental import pallas as pl
from jax.experimental.pallas import tpu as pltpu

D_MODEL = 2048
N_HEADS = 16
HEAD_DIM = 128
MOBA_BLOCK = 256
MOBA_TOP_K = 3
CHUNK = 128
D_GMLP = 2 * D_MODEL
N_SGU_GROUPS = 16
SGU_GROUP = D_GMLP // N_SGU_GROUPS
D_FF = 4 * D_MODEL
NORM_EPS = 1e-6

V7X_VMEM_LIMIT_BYTES = 56 * 1024 * 1024

F32 = jnp.float32
BF16 = jnp.bfloat16
NEG_INF = float("-inf")
LOG2E = 1.4426950408889634


def _params(semantics):
    return pltpu.CompilerParams(dimension_semantics=semantics,
                                vmem_limit_bytes=V7X_VMEM_LIMIT_BYTES)


def _rmsnorm_rows(x, g):
    return x * lax.rsqrt(jnp.mean(x * x, axis=-1, keepdims=True) + NORM_EPS) * g


def _dot(a, b):
    return jnp.dot(a, b, preferred_element_type=F32)


def _dot_nt(a, b, precision=None):
    return lax.dot_general(a, b, (((1,), (1,)), ((), ())), precision=precision,
                           preferred_element_type=F32)


def _weight_spec(w, layer, block, index_map):
    mode = {"pipeline_mode": pl.Buffered(1)} if tuple(block) == tuple(w.shape[-2:]) else {}
    if w.ndim == 2:
        return pl.BlockSpec(block, index_map, **mode)
    return pl.BlockSpec((None,) + block, lambda *args: (layer,) + tuple(index_map(*args)), **mode)


def _use_weight(w_ref, wb_ref):
    w = w_ref[...].astype(BF16)
    if wb_ref is not None:
        wb_ref[...] = w
    return w


def _norm_matmul_kernel(*refs, n_w, emit_bf16):
    x_ref, g_ref = refs[:2]
    w_refs = refs[2:2 + n_w]
    o_refs = refs[2 + n_w:2 + 2 * n_w]
    wb_refs = refs[2 + 2 * n_w:2 + 3 * n_w] if emit_bf16 else [None] * n_w
    xn_ref = refs[-1]

    @pl.when(pl.program_id(1) == 0)
    def _():
        xn_ref[...] = _rmsnorm_rows(x_ref[...], g_ref[...]).astype(BF16)

    xn = xn_ref[...]
    for w_ref, o_ref, wb_ref in zip(w_refs, o_refs, wb_refs):
        o_ref[...] = _dot(xn, _use_weight(w_ref, wb_ref))


def _norm_matmul(x, g, ws, *, tm, tn, layer=0, emit_bf16=False):
    rows, d = x.shape
    n = ws[0].shape[-1]
    n_w = len(ws)
    out_specs = [pl.BlockSpec((tm, tn), lambda i, j: (i, j)) for _ in ws]
    out_shape = [jax.ShapeDtypeStruct((rows, n), F32) for _ in ws]
    if emit_bf16:
        assert rows == tm
        out_specs += [pl.BlockSpec((d, tn), lambda i, j: (0, j)) for _ in ws]
        out_shape += [jax.ShapeDtypeStruct((d, n), BF16) for _ in ws]
    return pl.pallas_call(
        functools.partial(_norm_matmul_kernel, n_w=n_w, emit_bf16=emit_bf16),
        grid=(rows // tm, n // tn),
        in_specs=[pl.BlockSpec((tm, d), lambda i, j: (i, 0)),
                  pl.BlockSpec((1, d), lambda i, j: (0, 0))]
                 + [_weight_spec(w, layer, (d, tn), lambda i, j: (0, j)) for w in ws],
        out_specs=out_specs,
        out_shape=out_shape,
        scratch_shapes=[pltpu.VMEM((tm, d), BF16)],
        compiler_params=_params(("parallel", "arbitrary")),
        name="norm_matmul",
    )(x, g.reshape(1, d), *ws)


def _proj_residual_kernel(a_ref, w_ref, x_ref, o_ref, wb_ref=None):
    o_ref[...] = x_ref[...] + _dot(a_ref[...].astype(BF16), _use_weight(w_ref, wb_ref))


def _proj_residual(a, w, x, *, tm, tn, layer=0, emit_bf16=False):
    rows, k = a.shape
    n = w.shape[-1]
    out_specs = [pl.BlockSpec((tm, tn), lambda i, j: (i, j))]
    out_shape = [jax.ShapeDtypeStruct((rows, n), F32)]
    if emit_bf16:
        assert rows == tm
        out_specs.append(pl.BlockSpec((k, tn), lambda i, j: (0, j)))
        out_shape.append(jax.ShapeDtypeStruct((k, n), BF16))
    return pl.pallas_call(
        _proj_residual_kernel,
        grid=(rows // tm, n // tn),
        in_specs=[pl.BlockSpec((tm, k), lambda i, j: (i, 0)),
                  _weight_spec(w, layer, (k, tn), lambda i, j: (0, j)),
                  pl.BlockSpec((tm, tn), lambda i, j: (i, j))],
        out_specs=out_specs,
        out_shape=out_shape,
        compiler_params=_params(("parallel", "parallel")),
        name="proj_residual",
    )(a, w, x)


def _ffn_kernel(x_ref, g_ref, w1_ref, w2_ref, gf_ref, o_ref, *rest, final_norm, emit_bf16):
    w1b_ref, w2b_ref = rest[:2] if emit_bf16 else (None, None)
    xn_ref = rest[-1]
    j = pl.program_id(1)

    @pl.when(j == 0)
    def _():
        x = x_ref[...]
        xn_ref[...] = _rmsnorm_rows(x, g_ref[...]).astype(BF16)
        o_ref[...] = x

    h = jnp.maximum(_dot(xn_ref[...], _use_weight(w1_ref, w1b_ref)), 0.0)
    o_ref[...] += _dot((h * h).astype(BF16), _use_weight(w2_ref, w2b_ref))

    if final_norm:
        @pl.when(j == pl.num_programs(1) - 1)
        def _():
            o_ref[...] = _rmsnorm_rows(o_ref[...], gf_ref[...])


def _ffn(x, g, w1, w2, gf, *, tm, tf, final_norm, layer=0, emit_bf16=False):
    rows, d = x.shape
    dff = w1.shape[-1]
    out_specs = [pl.BlockSpec((tm, d), lambda i, j: (i, 0))]
    out_shape = [jax.ShapeDtypeStruct((rows, d), F32)]
    if emit_bf16:
        assert rows == tm
        out_specs += [pl.BlockSpec((d, tf), lambda i, j: (0, j)),
                      pl.BlockSpec((tf, d), lambda i, j: (j, 0))]
        out_shape += [jax.ShapeDtypeStruct((d, dff), BF16), jax.ShapeDtypeStruct((dff, d), BF16)]
    return pl.pallas_call(
        functools.partial(_ffn_kernel, final_norm=final_norm, emit_bf16=emit_bf16),
        grid=(rows // tm, dff // tf),
        in_specs=[pl.BlockSpec((tm, d), lambda i, j: (i, 0)),
                  pl.BlockSpec((1, d), lambda i, j: (0, 0)),
                  _weight_spec(w1, layer, (d, tf), lambda i, j: (0, j)),
                  _weight_spec(w2, layer, (tf, d), lambda i, j: (j, 0)),
                  pl.BlockSpec((1, d), lambda i, j: (0, 0))],
        out_specs=out_specs,
        out_shape=out_shape,
        scratch_shapes=[pltpu.VMEM((tm, d), BF16)],
        compiler_params=_params(("parallel", "arbitrary")),
        name="ffn",
    )(x, g.reshape(1, d), w1, w2, gf.reshape(1, d))


def _gmlp_kernel(x_ref, g_ref, win_ref, lng_ref, lnb_ref, ws_ref, bs_ref, wout_ref,
                 y_ref, vout_ref, *rest, nv, tn, tk, chunk_len, mix_rows, emit_bf16):
    winb_ref, woutb_ref = rest[:2] if emit_bf16 else (None, None)
    xn_ref, v_ref, h_ref, wsb_ref = rest[-4:]
    s = pl.program_id(1)
    tm = x_ref.shape[0]

    @pl.when(s == 0)
    def _():
        x = x_ref[...]
        xn_ref[...] = _rmsnorm_rows(x, g_ref[...]).astype(BF16)
        y_ref[...] = x

    @pl.when(s < nv)
    def _():
        z = jax.nn.gelu(_dot(xn_ref[...], _use_weight(win_ref, winb_ref)))
        v_ref[:, pl.ds(pl.multiple_of(s * tn, tn), tn)] = z

    @pl.when(s == nv)
    def _():
        r = lax.broadcasted_iota(jnp.int32, (mix_rows, mix_rows), 0)
        c = lax.broadcasted_iota(jnp.int32, (mix_rows, mix_rows), 1)
        keep = (r // chunk_len == c // chunk_len) & (c <= r)
        for grp in range(N_SGU_GROUPS):
            wsb_ref[grp] = jnp.where(keep, ws_ref[grp], 0.0).astype(BF16)
        ln_g = lng_ref[...]
        ln_b = lnb_ref[...]

        def mix_tile(t, carry):
            rows = pl.ds(pl.multiple_of(t * mix_rows, mix_rows), mix_rows)
            v = v_ref[rows, :]
            mu = jnp.mean(v, axis=-1, keepdims=True)
            vc = v - mu
            var = jnp.mean(vc * vc, axis=-1, keepdims=True)
            vn = vc * lax.rsqrt(var + NORM_EPS) * ln_g + ln_b
            vout_ref[...] = vn
            for grp in range(N_SGU_GROUPS):
                cols = slice(grp * SGU_GROUP, (grp + 1) * SGU_GROUP)
                v_ref[rows, cols] = (_dot(wsb_ref[grp], vn[:, cols].astype(BF16))
                                     + bs_ref[:, grp:grp + 1])
            return carry

        lax.fori_loop(0, tm // mix_rows, mix_tile, 0)

    @pl.when((s >= nv) & (s < 2 * nv))
    def _():
        cols = pl.ds(pl.multiple_of((s - nv) * tn, tn), tn)
        u = jax.nn.gelu(_dot(xn_ref[...], _use_weight(win_ref, winb_ref)))
        h_ref[:, cols] = (u * v_ref[:, cols]).astype(BF16)

    @pl.when(s >= 2 * nv)
    def _():
        k = s - 2 * nv
        h = h_ref[:, pl.ds(pl.multiple_of(k * tk, tk), tk)]
        y_ref[...] += _dot(h, _use_weight(wout_ref, woutb_ref))


def _gmlp(x, g, w_in, ln_g, ln_b, ws_tiled, bs_tiled, w_out, *, tm, tn, tk, chunk_len,
          v_period, layer=0, emit_bf16=False):
    rows, d = x.shape
    mix_rows = ws_tiled.shape[1]
    nv = D_GMLP // tn
    n_out = D_GMLP // tk
    n_vblocks = rows // (tm * v_period)
    win_tile = lambda i, s: (0, jnp.where(s < nv, s + nv, jnp.minimum(s - nv, nv - 1)))
    wout_tile = lambda i, s: (jnp.maximum(s - 2 * nv, 0), 0)
    out_specs = [pl.BlockSpec((tm, d), lambda i, s: (i, 0)),
                 pl.BlockSpec((mix_rows, D_GMLP), lambda i, s: (i // v_period, 0))]
    out_shape = [jax.ShapeDtypeStruct((rows, d), F32),
                 jax.ShapeDtypeStruct((n_vblocks * mix_rows, D_GMLP), F32)]
    if emit_bf16:
        assert rows == tm
        out_specs += [pl.BlockSpec((d, tn), win_tile), pl.BlockSpec((tk, d), wout_tile)]
        out_shape += [jax.ShapeDtypeStruct((d, 2 * D_GMLP), BF16),
                      jax.ShapeDtypeStruct((D_GMLP, d), BF16)]
    kern = functools.partial(_gmlp_kernel, nv=nv, tn=tn, tk=tk, chunk_len=chunk_len,
                             mix_rows=mix_rows, emit_bf16=emit_bf16)
    return pl.pallas_call(
        kern,
        grid=(rows // tm, 2 * nv + n_out),
        in_specs=[pl.BlockSpec((tm, d), lambda i, s: (i, 0)),
                  pl.BlockSpec((1, d), lambda i, s: (0, 0)),
                  _weight_spec(w_in, layer, (d, tn), win_tile),
                  pl.BlockSpec((1, D_GMLP), lambda i, s: (0, 0)),
                  pl.BlockSpec((1, D_GMLP), lambda i, s: (0, 0)),
                  pl.BlockSpec((N_SGU_GROUPS, mix_rows, mix_rows), lambda i, s: (0, 0, 0)),
                  pl.BlockSpec((mix_rows, N_SGU_GROUPS), lambda i, s: (0, 0)),
                  _weight_spec(w_out, layer, (tk, d), wout_tile)],
        out_specs=out_specs,
        out_shape=out_shape,
        scratch_shapes=[pltpu.VMEM((tm, d), BF16),
                        pltpu.VMEM((tm, D_GMLP), F32),
                        pltpu.VMEM((tm, D_GMLP), BF16),
                        pltpu.VMEM((N_SGU_GROUPS, mix_rows, mix_rows), BF16)],
        compiler_params=_params(("arbitrary", "arbitrary")),
        name="gmlp",
    )(x, g.reshape(1, d), w_in, ln_g.reshape(1, D_GMLP), ln_b.reshape(1, D_GMLP),
      ws_tiled, bs_tiled, w_out)


def _top_blocks(gate, blk, n_take, axis):
    nb = gate.shape[axis]
    picks = []
    sel = jnp.zeros_like(gate)
    for r in range(MOBA_TOP_K):
        m = jnp.max(gate, axis=axis, keepdims=True)
        idx = jnp.min(jnp.where(gate == m, blk, float(nb)), axis=axis, keepdims=True)
        hit = blk == idx
        picks.append(idx)
        counts = 1.0 if n_take is None else jnp.where(r < n_take, 1.0, 0.0)
        sel = jnp.maximum(sel, jnp.where(hit, counts, 0.0))
        gate = jnp.where(hit, NEG_INF, gate)
    return picks, sel


def _kv_proj_kernel(x_ref, g_ref, wk_ref, wv_ref, k_ref, v_ref, kb_ref, vb_ref, kmean_ref, xn_ref):
    @pl.when(pl.program_id(1) == 0)
    def _():
        xn_ref[...] = _rmsnorm_rows(x_ref[...], g_ref[...]).astype(BF16)

    xn = xn_ref[...]
    k = _dot(xn, wk_ref[...])
    v = _dot(xn, wv_ref[...])
    k_ref[...] = k
    v_ref[...] = v
    kb_ref[...] = k.astype(BF16)
    vb_ref[...] = v.astype(BF16)
    for blk in range(kmean_ref.shape[0]):
        rows = slice(blk * MOBA_BLOCK, (blk + 1) * MOBA_BLOCK)
        kmean_ref[blk] = jnp.mean(k[rows, :], axis=0, keepdims=True)


def _kv_proj(x, g, wk, wv, *, tm, tn):
    rows, d = x.shape
    n = wk.shape[1]
    blocks_per_tile = tm // MOBA_BLOCK
    tile = lambda: pl.BlockSpec((tm, tn), lambda i, j: (i, j))
    return pl.pallas_call(
        _kv_proj_kernel,
        grid=(rows // tm, n // tn),
        in_specs=[pl.BlockSpec((tm, d), lambda i, j: (i, 0)),
                  pl.BlockSpec((1, d), lambda i, j: (0, 0)),
                  _weight_spec(wk, 0, (d, tn), lambda i, j: (0, j)),
                  _weight_spec(wv, 0, (d, tn), lambda i, j: (0, j))],
        out_specs=[tile(), tile(), tile(), tile(),
                   pl.BlockSpec((blocks_per_tile, 1, tn), lambda i, j: (i, 0, j))],
        out_shape=[jax.ShapeDtypeStruct((rows, n), F32), jax.ShapeDtypeStruct((rows, n), F32),
                   jax.ShapeDtypeStruct((rows, n), BF16), jax.ShapeDtypeStruct((rows, n), BF16),
                   jax.ShapeDtypeStruct((rows // MOBA_BLOCK, 1, n), F32)],
        scratch_shapes=[pltpu.VMEM((tm, d), BF16)],
        compiler_params=_params(("parallel", "arbitrary")),
        name="kv_proj",
    )(x, g.reshape(1, d), wk, wv)


HEADS_PER_STEP = 8
BLOCKS_PER_TRIP = 4


def _moba_prompt_kernel(slopes_ref, q_ref, kb_ref, vb_ref, kmean_ref, o_ref,
                        vt_ref, qaug_ref, kaug_ref, sel_ref, acc_ref):
    hg = pl.program_id(1)
    qt = pl.program_id(2)
    n_blocks = kb_ref.shape[1] // MOBA_BLOCK
    score_scale = (HEAD_DIM ** -0.5) * LOG2E
    heads = range(HEADS_PER_STEP)
    cols = [slice(hh * HEAD_DIM, (hh + 1) * HEAD_DIM) for hh in heads]
    slope2 = [slopes_ref[hg * HEADS_PER_STEP + hh] * LOG2E for hh in heads]

    @pl.when(qt == 0)
    def _():
        for j in range(n_blocks):
            rows = slice(j * MOBA_BLOCK, (j + 1) * MOBA_BLOCK)
            vt_ref[:, rows] = vb_ref[0, rows, :].astype(F32).T.astype(BF16)
        key_i = lax.broadcasted_iota(jnp.int32, (MOBA_BLOCK, HEAD_DIM), 0)
        piece = lax.broadcasted_iota(jnp.int32, (MOBA_BLOCK, HEAD_DIM), 1)
        kaug_ref[...] = jnp.where(piece < 3, key_i, 0).astype(BF16)
        part = lax.broadcasted_iota(jnp.int32, (HEAD_DIM, MOBA_BLOCK), 0)
        for hh in heads:
            whole = jnp.full((HEAD_DIM, MOBA_BLOCK), slope2[hh], F32)
            hi = whole.astype(BF16).astype(F32)
            mid = (whole - hi).astype(BF16).astype(F32)
            lo = whole - hi - mid
            pieces = jnp.where(part == 0, hi, jnp.where(part == 1, mid, jnp.where(part == 2, lo, 0.0)))
            qaug_ref[hh, HEAD_DIM:, :] = pieces.astype(BF16)

    key_i = lax.broadcasted_iota(jnp.int32, (MOBA_BLOCK, MOBA_BLOCK), 0)
    qry_i = lax.broadcasted_iota(jnp.int32, (MOBA_BLOCK, MOBA_BLOCK), 1)
    causal = qry_i >= key_i
    own = pl.ds(pl.multiple_of(qt * MOBA_BLOCK, MOBA_BLOCK), MOBA_BLOCK)
    q_ts = [q_ref[0, :, cols[hh]].T for hh in heads]
    for hh in heads:
        qaug_ref[hh, :HEAD_DIM, :] = (q_ts[hh] * score_scale).astype(BF16)
    kaug = kaug_ref[...]

    def scores(rows, hh):
        keys = jnp.concatenate([kb_ref[0, rows, cols[hh]], kaug], axis=1)
        return _dot(keys, qaug_ref[hh])

    raw = [scores(own, hh) for hh in heads]
    gates = [jnp.dot(kmean_ref[0, :, cols[hh]], q_ts[hh], precision=lax.Precision.HIGHEST,
                     preferred_element_type=F32) for hh in heads]
    stats = []
    for hh in heads:
        blk = lax.broadcasted_iota(jnp.int32, gates[hh].shape, 0)
        gate = jnp.where(blk < qt, gates[hh], NEG_INF)
        _, sel = _top_blocks(gate, blk.astype(F32), qt, axis=0)
        sel_ref[hh] = sel
        t = jnp.where(causal, raw[hh], NEG_INF)
        m0 = jnp.max(t, axis=0, keepdims=True)
        p = jnp.exp2(t - m0)
        l0 = jnp.sum(p, axis=0, keepdims=True)
        acc_ref[hh] = _dot(vt_ref[cols[hh], own], p.astype(BF16))
        stats.append((m0, l0))

    def past_blocks(j0, n_now, stats):
        blocks = range(n_now)
        rows = [pl.ds(pl.multiple_of((j0 + i) * MOBA_BLOCK, MOBA_BLOCK), MOBA_BLOCK) for i in blocks]
        raw = [[scores(rows[i], hh) for hh in heads] for i in blocks]
        stats = list(stats)
        for i in blocks:
            blocks_between = ((qt - j0 - i) * MOBA_BLOCK).astype(F32)
            for hh in heads:
                m, l = stats[hh]
                t = raw[i][hh]
                far = slope2[hh] * blocks_between
                picked = sel_ref[hh, pl.ds(j0 + i, 1), :] > 0.0
                m_blk = jnp.max(t, axis=0, keepdims=True) - far
                m_new = jnp.where(picked, jnp.maximum(m, m_blk), m)
                shift = jnp.where(picked, m_new + far, jnp.inf)
                p = jnp.exp2(t - shift)
                alpha = jnp.exp2(m - m_new)
                l = alpha * l + jnp.sum(p, axis=0, keepdims=True)
                acc_ref[hh] = alpha * acc_ref[hh] + _dot(vt_ref[cols[hh], rows[i]], p.astype(BF16))
                stats[hh] = (m_new, l)
        return tuple(stats)

    stats = lax.fori_loop(0, qt // BLOCKS_PER_TRIP,
                          lambda i, st: past_blocks(i * BLOCKS_PER_TRIP, BLOCKS_PER_TRIP, st),
                          tuple(stats))
    done = (qt // BLOCKS_PER_TRIP) * BLOCKS_PER_TRIP
    stats = lax.fori_loop(done, qt, lambda j, st: past_blocks(j, 1, st), stats)
    for hh in heads:
        o_ref[0, :, cols[hh]] = (acc_ref[hh] / stats[hh][1]).T.astype(o_ref.dtype)


def _moba_prompt(q, kb, vb, kmean, slopes):
    bsz, seq, _ = q.shape
    n_qt = seq // MOBA_BLOCK
    width = HEADS_PER_STEP * HEAD_DIM
    return pl.pallas_call(
        _moba_prompt_kernel,
        grid_spec=pltpu.PrefetchScalarGridSpec(
            num_scalar_prefetch=0,
            grid=(bsz, N_HEADS // HEADS_PER_STEP, n_qt),
            in_specs=[pl.BlockSpec(memory_space=pltpu.SMEM),
                      pl.BlockSpec((1, MOBA_BLOCK, width), lambda b, h, t: (b, t, h)),
                      pl.BlockSpec((1, seq, width), lambda b, h, t: (b, 0, h),
                                   pipeline_mode=pl.Buffered(1)),
                      pl.BlockSpec((1, seq, width), lambda b, h, t: (b, 0, h),
                                   pipeline_mode=pl.Buffered(1)),
                      pl.BlockSpec((1, n_qt, width), lambda b, h, t: (b, 0, h))],
            out_specs=pl.BlockSpec((1, MOBA_BLOCK, width), lambda b, h, t: (b, t, h)),
            scratch_shapes=[pltpu.VMEM((width, seq), BF16),
                            pltpu.VMEM((HEADS_PER_STEP, 2 * HEAD_DIM, MOBA_BLOCK), BF16),
                            pltpu.VMEM((MOBA_BLOCK, HEAD_DIM), BF16),
                            pltpu.VMEM((HEADS_PER_STEP, n_qt, MOBA_BLOCK), F32),
                            pltpu.VMEM((HEADS_PER_STEP, HEAD_DIM, MOBA_BLOCK), F32)]),
        out_shape=jax.ShapeDtypeStruct(q.shape, BF16),
        compiler_params=_params(("parallel", "parallel", "arbitrary")),
        name="moba_prompt",
    )(slopes, q, kb, vb, kmean)


MEANS_BLOCKS_PER_STEP = 4


def _cached_means_kernel(pt_ref, *refs, ppb):
    del pt_ref
    o_ref = refs[-1]
    page_refs = refs[:-1]
    for blk in range(MEANS_BLOCKS_PER_STEP):
        total = jnp.sum(page_refs[blk * ppb][0], axis=0)
        for p in range(1, ppb):
            total = total + jnp.sum(page_refs[blk * ppb + p][0], axis=0)
        o_ref[0, blk] = total / MOBA_BLOCK


def _cached_means(cache_k, pt_flat, db, n_pages):
    _, page, n_kv, dh = cache_k.shape
    ppb = MOBA_BLOCK // page
    nfb = (n_pages * page) // MOBA_BLOCK
    pages_per_step = MEANS_BLOCKS_PER_STEP * ppb
    assert nfb % MEANS_BLOCKS_PER_STEP == 0

    def page_spec(p):
        return pl.BlockSpec((1, page, n_kv, dh),
                            lambda b, n, pt: (pt[b * n_pages + n * pages_per_step + p], 0, 0, 0))

    return pl.pallas_call(
        functools.partial(_cached_means_kernel, ppb=ppb),
        grid_spec=pltpu.PrefetchScalarGridSpec(
            num_scalar_prefetch=1,
            grid=(db, nfb // MEANS_BLOCKS_PER_STEP),
            in_specs=[page_spec(p) for p in range(pages_per_step)],
            out_specs=pl.BlockSpec((1, MEANS_BLOCKS_PER_STEP, n_kv, dh),
                                   lambda b, n, pt: (b, n, 0, 0))),
        out_shape=jax.ShapeDtypeStruct((db, nfb, n_kv, dh), F32),
        compiler_params=_params(("parallel", "parallel")),
        name="cached_means",
    )(pt_flat, *([cache_k] * pages_per_step))


def _sample_topk_kernel(q_ref, km_ref, o_ref):
    t = q_ref.shape[1]
    lane = lax.broadcasted_iota(jnp.int32, (t, 128), 1)
    for h in range(N_HEADS):
        cols = slice(h * HEAD_DIM, (h + 1) * HEAD_DIM)
        gate = _dot_nt(q_ref[0, :, cols], km_ref[0, h], precision=lax.Precision.HIGHEST)
        blk_f = lax.broadcasted_iota(jnp.int32, gate.shape, 1).astype(F32)
        picks, _ = _top_blocks(gate, blk_f, None, axis=1)
        out = jnp.zeros((t, 128), F32)
        for r, idx in enumerate(picks):
            out = jnp.where(lane == r, idx, out)
        o_ref[0, h] = out.astype(jnp.int32)


def _sample_topk(q, kmean):
    db, t, _ = q.shape
    nfb = kmean.shape[2]
    assert nfb >= MOBA_TOP_K
    out = pl.pallas_call(
        _sample_topk_kernel,
        grid=(db,),
        in_specs=[pl.BlockSpec((1, t, D_MODEL), lambda b: (b, 0, 0)),
                  pl.BlockSpec((1, N_HEADS, nfb, HEAD_DIM), lambda b: (b, 0, 0, 0))],
        out_specs=pl.BlockSpec((1, N_HEADS, t, 128), lambda b: (b, 0, 0, 0)),
        out_shape=jax.ShapeDtypeStruct((db, N_HEADS, t, 128), jnp.int32),
        compiler_params=_params(("parallel",)),
        name="sample_topk",
    )(q, kmean)
    return out[..., :MOBA_TOP_K]


def _moba_sample_kernel(idx_ref, pt_ref, slopes_ref, q_ref, kn_ref, vn_ref, ck_hbm, cv_hbm,
                        o_ref, kbuf, vbuf, sems, *, n_sel_pages, ppb, n_pages):
    b = pl.program_id(0)
    h = pl.program_id(1)
    n_t = q_ref.shape[1]
    page = kbuf.shape[2]
    past_len = n_pages * page
    step = b * N_HEADS + h
    n_steps = pl.num_programs(0) * N_HEADS
    slot = step % 2
    scale = HEAD_DIM ** -0.5
    slope = slopes_ref[h]

    def logical_page(bb, hh, t, n):
        return idx_ref[((bb * N_HEADS + hh) * n_t + t) * MOBA_TOP_K + n // ppb] * ppb + n % ppb

    def page_copies(bb, hh, sl):
        copies = []
        for t in range(n_t):
            for n in range(n_sel_pages):
                phys = pt_ref[bb * n_pages + logical_page(bb, hh, t, n)]
                dst = t * n_sel_pages + n
                copies.append(pltpu.make_async_copy(ck_hbm.at[phys, :, hh, :], kbuf.at[sl, dst],
                                                    sems.at[0, sl]))
                copies.append(pltpu.make_async_copy(cv_hbm.at[phys, :, hh, :], vbuf.at[sl, dst],
                                                    sems.at[1, sl]))
        return copies

    @pl.when(step == 0)
    def _():
        for cp in page_copies(b, h, slot):
            cp.start()

    @pl.when(step + 1 < n_steps)
    def _():
        nxt = step + 1
        for cp in page_copies(nxt // N_HEADS, nxt % N_HEADS, 1 - slot):
            cp.start()

    for cp in page_copies(b, h, slot):
        cp.wait()

    off_bias = slope * lax.broadcasted_iota(jnp.int32, (page, 1), 0).astype(F32)
    new_i = lax.broadcasted_iota(jnp.int32, (n_t, 1), 0)

    def one_token(t, carry):
        q = q_ref[0, pl.ds(t, 1), :]
        pos_q = past_len + t
        scores = []
        for n in range(n_sel_pages):
            s = jnp.sum(kbuf[slot, t * n_sel_pages + n] * q, axis=1, keepdims=True) * scale
            to_page = (pos_q - logical_page(b, h, t, n) * page).astype(F32)
            scores.append(s + off_bias - slope * to_page)
        s_new = jnp.sum(kn_ref[0] * q, axis=1, keepdims=True) * scale
        d_new = t - new_i
        s_new = jnp.where(d_new >= 0, s_new - slope * d_new.astype(F32), NEG_INF)

        m = jnp.max(s_new, axis=0, keepdims=True)
        for s in scores:
            m = jnp.maximum(m, jnp.max(s, axis=0, keepdims=True))
        p_new = jnp.exp(s_new - m)
        l = jnp.sum(p_new, axis=0, keepdims=True)
        acc = jnp.sum(p_new * vn_ref[0], axis=0, keepdims=True)
        for n, s in enumerate(scores):
            p = jnp.exp(s - m)
            l = l + jnp.sum(p, axis=0, keepdims=True)
            acc = acc + jnp.sum(p * vbuf[slot, t * n_sel_pages + n], axis=0, keepdims=True)
        o_ref[0, pl.ds(t, 1), :] = acc / l
        return carry

    lax.fori_loop(0, n_t, one_token, 0)


def _moba_sample(q, k_new, v_new, cache_k, cache_v, idx_flat, pt_flat, slopes, n_pages):
    db, t, _ = q.shape
    page = cache_k.shape[1]
    ppb = MOBA_BLOCK // page
    n_sel_pages = MOBA_TOP_K * ppb

    def new_spec():
        return pl.BlockSpec((1, t, HEAD_DIM), lambda b, h, idx, pt: (b, 0, h))

    kern = functools.partial(_moba_sample_kernel, n_sel_pages=n_sel_pages, ppb=ppb,
                             n_pages=n_pages)
    return pl.pallas_call(
        kern,
        grid_spec=pltpu.PrefetchScalarGridSpec(
            num_scalar_prefetch=2,
            grid=(db, N_HEADS),
            in_specs=[pl.BlockSpec(memory_space=pltpu.SMEM), new_spec(), new_spec(), new_spec(),
                      pl.BlockSpec(memory_space=pl.ANY), pl.BlockSpec(memory_space=pl.ANY)],
            out_specs=pl.BlockSpec((1, t, HEAD_DIM), lambda b, h, idx, pt: (b, 0, h)),
            scratch_shapes=[pltpu.VMEM((2, t * n_sel_pages, page, HEAD_DIM), F32),
                            pltpu.VMEM((2, t * n_sel_pages, page, HEAD_DIM), F32),
                            pltpu.SemaphoreType.DMA((2, 2))]),
        out_shape=jax.ShapeDtypeStruct(q.shape, F32),
        compiler_params=_params(("arbitrary", "arbitrary")),
        name="moba_sample",
    )(idx_flat, pt_flat, slopes, q, k_new, v_new, cache_k, cache_v)


def kernel(x_prompt, x_sample, cache_k, cache_v, page_table, norm_mix_g, norm_ffn_g,
           gmlp_w_in, gmlp_ln_g, gmlp_ln_b, gmlp_w_s, gmlp_b_s, gmlp_w_out,
           kv_norm_g, w_k, w_v, attn_w_q, attn_w_o, ffn_w1, ffn_w2, final_norm_g):
    bp, sp, d = x_prompt.shape
    bs, ss, _ = x_sample.shape
    depth = norm_mix_g.shape[0]
    n_a = gmlp_w_in.shape[0]
    _, page, n_kv, dh = cache_k.shape
    n_pages = page_table.shape[1]
    assert sp % MOBA_BLOCK == 0 and sp % CHUNK == 0 and ss <= CHUNK
    assert (n_pages * page) % MOBA_BLOCK == 0
    assert n_kv == N_HEADS and dh == HEAD_DIM

    slopes = jnp.exp2(-8.0 * jnp.arange(1, N_HEADS + 1, dtype=F32) / N_HEADS)

    rows_p, rows_s = bp * sp, bs * ss
    xp = x_prompt.reshape(rows_p, d)
    xs = x_sample.reshape(rows_s, d)
    pt_flat = page_table.reshape(-1)

    tm_p = 512
    sgu_p, sgu_s = [], []
    for layer in range(depth):
        g = norm_mix_g[layer]
        if layer < n_a:
            ws_p = gmlp_w_s[layer]
            bs_p = gmlp_b_s[layer].T
            ws_s = jnp.tile(gmlp_w_s[layer][:, :ss, :ss], (1, bs, bs))
            bs_s = jnp.tile(gmlp_b_s[layer][:, :ss].T, (bs, 1))
            xs, vs, w_in, w_out = _gmlp(
                xs, g, gmlp_w_in, gmlp_ln_g[layer], gmlp_ln_b[layer], ws_s, bs_s, gmlp_w_out,
                tm=rows_s, tn=512, tk=512, chunk_len=ss, v_period=1, layer=layer, emit_bf16=True)
            xp, vp = _gmlp(xp, g, w_in, gmlp_ln_g[layer], gmlp_ln_b[layer], ws_p, bs_p, w_out,
                           tm=tm_p, tn=1024, tk=512, chunk_len=CHUNK, v_period=sp // tm_p)
            sgu_p.append(vp.reshape(bp, CHUNK, D_GMLP))
            sgu_s.append(vs.reshape(bs, ss, D_GMLP))
        else:
            if layer == n_a:
                ks_, vs_, wk_b, wv_b = _norm_matmul(xs, kv_norm_g, [w_k, w_v], tm=rows_s, tn=512,
                                                   emit_bf16=True)
                kp, vp_, kb_p, vb_p, kmean_p = _kv_proj(xp, kv_norm_g, wk_b, wv_b, tm=256, tn=d)
                kmean_s = _cached_means(cache_k, pt_flat, bs, n_pages).transpose(0, 2, 1, 3)
            a = layer - n_a
            qs, wq_b = _norm_matmul(xs, g, [attn_w_q], tm=rows_s, tn=512, layer=a, emit_bf16=True)
            qs3 = qs.reshape(bs, ss, d)
            idx = _sample_topk(qs3, kmean_s)
            att_s = _moba_sample(qs3, ks_.reshape(bs, ss, d), vs_.reshape(bs, ss, d),
                                 cache_k, cache_v, idx.reshape(-1), pt_flat, slopes, n_pages)
            xs, wo_b = _proj_residual(att_s.reshape(rows_s, d), attn_w_o, xs, tm=rows_s, tn=512,
                                      layer=a, emit_bf16=True)

            (qp,) = _norm_matmul(xp, g, [wq_b], tm=tm_p, tn=d)
            att_p = _moba_prompt(qp.reshape(bp, sp, d), kb_p.reshape(bp, sp, d),
                                 vb_p.reshape(bp, sp, d),
                                 kmean_p.reshape(bp, sp // MOBA_BLOCK, d), slopes)
            (xp,) = _proj_residual(att_p.reshape(rows_p, d), wo_b, xp, tm=tm_p, tn=d)
        last = layer == depth - 1
        xs, w1_b, w2_b = _ffn(xs, norm_ffn_g[layer], ffn_w1, ffn_w2, final_norm_g, tm=rows_s,
                              tf=512, final_norm=last, layer=layer, emit_bf16=True)
        (xp,) = _ffn(xp, norm_ffn_g[layer], w1_b, w2_b, final_norm_g, tm=tm_p, tf=1024,
                     final_norm=last)

    kv_shape_p = (bp, sp, n_kv, dh)
    kv_shape_s = (bs, ss, n_kv, dh)
    return (xp.reshape(bp, sp, d), xs.reshape(bs, ss, d),
            kp.reshape(kv_shape_p), vp_.reshape(kv_shape_p),
            ks_.reshape(kv_shape_s), vs_.reshape(kv_shape_s),
            jnp.stack(sgu_p), jnp.stack(sgu_s))
```

```python
import functools

import jax
import jax.numpy as jnp
from jax import lax
from jax.experimental import pallas as pl
from jax.experimental.pallas import tpu as pltpu

D_MODEL = 2048
N_HEADS = 16
HEAD_DIM = 128
MOBA_BLOCK = 256
MOBA_TOP_K = 3
CHUNK = 128
D_GMLP = 2 * D_MODEL
N_SGU_GROUPS = 16
SGU_GROUP = D_GMLP // N_SGU_GROUPS
D_FF = 4 * D_MODEL
NORM_EPS = 1e-6

V7X_VMEM_LIMIT_BYTES = 56 * 1024 * 1024

F32 = jnp.float32
BF16 = jnp.bfloat16
NEG_INF = float("-inf")
LOG2E = 1.4426950408889634


def _params(semantics):
    return pltpu.CompilerParams(dimension_semantics=semantics,
                                vmem_limit_bytes=V7X_VMEM_LIMIT_BYTES)


def _rmsnorm_rows(x, g):
    return x * lax.rsqrt(jnp.mean(x * x, axis=-1, keepdims=True) + NORM_EPS) * g


def _dot(a, b):
    return jnp.dot(a, b, preferred_element_type=F32)


def _dot_nt(a, b, precision=None):
    return lax.dot_general(a, b, (((1,), (1,)), ((), ())), precision=precision,
                           preferred_element_type=F32)


def _weight_spec(w, layer, block, index_map):
    mode = {"pipeline_mode": pl.Buffered(1)} if tuple(block) == tuple(w.shape[-2:]) else {}
    if w.ndim == 2:
        return pl.BlockSpec(block, index_map, **mode)
    return pl.BlockSpec((None,) + block, lambda *args: (layer,) + tuple(index_map(*args)), **mode)


def _use_weight(w_ref, wb_ref):
    w = w_ref[...].astype(BF16)
    if wb_ref is not None:
        wb_ref[...] = w
    return w


def _norm_matmul_kernel(*refs, n_w, emit_bf16):
    x_ref, g_ref = refs[:2]
    w_refs = refs[2:2 + n_w]
    o_refs = refs[2 + n_w:2 + 2 * n_w]
    wb_refs = refs[2 + 2 * n_w:2 + 3 * n_w] if emit_bf16 else [None] * n_w
    xn_ref = refs[-1]

    @pl.when(pl.program_id(1) == 0)
    def _():
        xn_ref[...] = _rmsnorm_rows(x_ref[...], g_ref[...]).astype(BF16)

    xn = xn_ref[...]
    for w_ref, o_ref, wb_ref in zip(w_refs, o_refs, wb_refs):
        o_ref[...] = _dot(xn, _use_weight(w_ref, wb_ref))


def _norm_matmul(x, g, ws, *, tm, tn, layer=0, emit_bf16=False):
    rows, d = x.shape
    n = ws[0].shape[-1]
    n_w = len(ws)
    out_specs = [pl.BlockSpec((tm, tn), lambda i, j: (i, j)) for _ in ws]
    out_shape = [jax.ShapeDtypeStruct((rows, n), F32) for _ in ws]
    if emit_bf16:
        assert rows == tm
        out_specs += [pl.BlockSpec((d, tn), lambda i, j: (0, j)) for _ in ws]
        out_shape += [jax.ShapeDtypeStruct((d, n), BF16) for _ in ws]
    return pl.pallas_call(
        functools.partial(_norm_matmul_kernel, n_w=n_w, emit_bf16=emit_bf16),
        grid=(rows // tm, n // tn),
        in_specs=[pl.BlockSpec((tm, d), lambda i, j: (i, 0)),
                  pl.BlockSpec((1, d), lambda i, j: (0, 0))]
                 + [_weight_spec(w, layer, (d, tn), lambda i, j: (0, j)) for w in ws],
        out_specs=out_specs,
        out_shape=out_shape,
        scratch_shapes=[pltpu.VMEM((tm, d), BF16)],
        compiler_params=_params(("parallel", "arbitrary")),
        name="norm_matmul",
    )(x, g.reshape(1, d), *ws)


def _proj_residual_kernel(a_ref, w_ref, x_ref, o_ref, wb_ref=None):
    o_ref[...] = x_ref[...] + _dot(a_ref[...].astype(BF16), _use_weight(w_ref, wb_ref))


def _proj_residual(a, w, x, *, tm, tn, layer=0, emit_bf16=False):
    rows, k = a.shape
    n = w.shape[-1]
    out_specs = [pl.BlockSpec((tm, tn), lambda i, j: (i, j))]
    out_shape = [jax.ShapeDtypeStruct((rows, n), F32)]
    if emit_bf16:
        assert rows == tm
        out_specs.append(pl.BlockSpec((k, tn), lambda i, j: (0, j)))
        out_shape.append(jax.ShapeDtypeStruct((k, n), BF16))
    return pl.pallas_call(
        _proj_residual_kernel,
        grid=(rows // tm, n // tn),
        in_specs=[pl.BlockSpec((tm, k), lambda i, j: (i, 0)),
                  _weight_spec(w, layer, (k, tn), lambda i, j: (0, j)),
                  pl.BlockSpec((tm, tn), lambda i, j: (i, j))],
        out_specs=out_specs,
        out_shape=out_shape,
        compiler_params=_params(("parallel", "parallel")),
        name="proj_residual",
    )(a, w, x)


def _ffn_kernel(x_ref, g_ref, w1_ref, w2_ref, gf_ref, o_ref, *rest, final_norm, emit_bf16):
    w1b_ref, w2b_ref = rest[:2] if emit_bf16 else (None, None)
    xn_ref = rest[-1]
    j = pl.program_id(1)

    @pl.when(j == 0)
    def _():
        x = x_ref[...]
        xn_ref[...] = _rmsnorm_rows(x, g_ref[...]).astype(BF16)
        o_ref[...] = x

    h = jnp.maximum(_dot(xn_ref[...], _use_weight(w1_ref, w1b_ref)), 0.0)
    o_ref[...] += _dot((h * h).astype(BF16), _use_weight(w2_ref, w2b_ref))

    if final_norm:
        @pl.when(j == pl.num_programs(1) - 1)
        def _():
            o_ref[...] = _rmsnorm_rows(o_ref[...], gf_ref[...])


def _ffn(x, g, w1, w2, gf, *, tm, tf, final_norm, layer=0, emit_bf16=False):
    rows, d = x.shape
    dff = w1.shape[-1]
    out_specs = [pl.BlockSpec((tm, d), lambda i, j: (i, 0))]
    out_shape = [jax.ShapeDtypeStruct((rows, d), F32)]
    if emit_bf16:
        assert rows == tm
        out_specs += [pl.BlockSpec((d, tf), lambda i, j: (0, j)),
                      pl.BlockSpec((tf, d), lambda i, j: (j, 0))]
        out_shape += [jax.ShapeDtypeStruct((d, dff), BF16), jax.ShapeDtypeStruct((dff, d), BF16)]
    return pl.pallas_call(
        functools.partial(_ffn_kernel, final_norm=final_norm, emit_bf16=emit_bf16),
        grid=(rows // tm, dff // tf),
        in_specs=[pl.BlockSpec((tm, d), lambda i, j: (i, 0)),
                  pl.BlockSpec((1, d), lambda i, j: (0, 0)),
                  _weight_spec(w1, layer, (d, tf), lambda i, j: (0, j)),
                  _weight_spec(w2, layer, (tf, d), lambda i, j: (j, 0)),
                  pl.BlockSpec((1, d), lambda i, j: (0, 0))],
        out_specs=out_specs,
        out_shape=out_shape,
        scratch_shapes=[pltpu.VMEM((tm, d), BF16)],
        compiler_params=_params(("parallel", "arbitrary")),
        name="ffn",
    )(x, g.reshape(1, d), w1, w2, gf.reshape(1, d))


def _gmlp_kernel(x_ref, g_ref, win_ref, lng_ref, lnb_ref, ws_ref, bs_ref, wout_ref,
                 y_ref, vout_ref, *rest, nv, tn, chunk_len, mix_rows, emit_bf16):
    winb_ref, woutb_ref = rest[:2] if emit_bf16 else (None, None)
    xn_ref, v_ref, wsb_ref = rest[-3:]
    s = pl.program_id(1)
    tm = x_ref.shape[0]

    @pl.when(s == 0)
    def _():
        x = x_ref[...]
        xn_ref[...] = _rmsnorm_rows(x, g_ref[...]).astype(BF16)
        y_ref[...] = x

    @pl.when(s < nv)
    def _():
        z = jax.nn.gelu(_dot(xn_ref[...], _use_weight(win_ref, winb_ref)))
        v_ref[:, pl.ds(pl.multiple_of(s * tn, tn), tn)] = z

    @pl.when(s == nv)
    def _():
        r = lax.broadcasted_iota(jnp.int32, (mix_rows, mix_rows), 0)
        c = lax.broadcasted_iota(jnp.int32, (mix_rows, mix_rows), 1)
        keep = (r // chunk_len == c // chunk_len) & (c <= r)
        for grp in range(N_SGU_GROUPS):
            wsb_ref[grp] = jnp.where(keep, ws_ref[grp], 0.0).astype(BF16)
        ln_g = lng_ref[...]
        ln_b = lnb_ref[...]

        def mix_tile(t, carry):
            rows = pl.ds(pl.multiple_of(t * mix_rows, mix_rows), mix_rows)
            v = v_ref[rows, :]
            mu = jnp.mean(v, axis=-1, keepdims=True)
            vc = v - mu
            var = jnp.mean(vc * vc, axis=-1, keepdims=True)
            vn = vc * lax.rsqrt(var + NORM_EPS) * ln_g + ln_b
            vout_ref[...] = vn
            for grp in range(N_SGU_GROUPS):
                cols = slice(grp * SGU_GROUP, (grp + 1) * SGU_GROUP)
                v_ref[rows, cols] = (_dot(wsb_ref[grp], vn[:, cols].astype(BF16))
                                     + bs_ref[:, grp:grp + 1])
            return carry

        lax.fori_loop(0, tm // mix_rows, mix_tile, 0)

    @pl.when(s >= nv)
    def _():
        cols = pl.ds(pl.multiple_of((s - nv) * tn, tn), tn)
        u = jax.nn.gelu(_dot(xn_ref[...], _use_weight(win_ref, winb_ref)))
        h = (u * v_ref[:, cols]).astype(BF16)
        y_ref[...] += _dot(h, _use_weight(wout_ref, woutb_ref))


def _gmlp(x, g, w_in, ln_g, ln_b, ws_tiled, bs_tiled, w_out, *, tm, tn, chunk_len,
          v_period, layer=0, emit_bf16=False):
    rows, d = x.shape
    mix_rows = ws_tiled.shape[1]
    nv = D_GMLP // tn
    n_vblocks = rows // (tm * v_period)
    win_tile = lambda i, s: (0, jnp.where(s < nv, s + nv, s - nv))
    wout_tile = lambda i, s: (jnp.maximum(s - nv, 0), 0)
    out_specs = [pl.BlockSpec((tm, d), lambda i, s: (i, 0)),
                 pl.BlockSpec((mix_rows, D_GMLP), lambda i, s: (i // v_period, 0))]
    out_shape = [jax.ShapeDtypeStruct((rows, d), F32),
                 jax.ShapeDtypeStruct((n_vblocks * mix_rows, D_GMLP), F32)]
    if emit_bf16:
        assert rows == tm
        out_specs += [pl.BlockSpec((d, tn), win_tile), pl.BlockSpec((tn, d), wout_tile)]
        out_shape += [jax.ShapeDtypeStruct((d, 2 * D_GMLP), BF16),
                      jax.ShapeDtypeStruct((D_GMLP, d), BF16)]
    kern = functools.partial(_gmlp_kernel, nv=nv, tn=tn, chunk_len=chunk_len,
                             mix_rows=mix_rows, emit_bf16=emit_bf16)
    return pl.pallas_call(
        kern,
        grid=(rows // tm, 2 * nv),
        in_specs=[pl.BlockSpec((tm, d), lambda i, s: (i, 0)),
                  pl.BlockSpec((1, d), lambda i, s: (0, 0)),
                  _weight_spec(w_in, layer, (d, tn), win_tile),
                  pl.BlockSpec((1, D_GMLP), lambda i, s: (0, 0)),
                  pl.BlockSpec((1, D_GMLP), lambda i, s: (0, 0)),
                  pl.BlockSpec((N_SGU_GROUPS, mix_rows, mix_rows), lambda i, s: (0, 0, 0)),
                  pl.BlockSpec((mix_rows, N_SGU_GROUPS), lambda i, s: (0, 0)),
                  _weight_spec(w_out, layer, (tn, d), wout_tile)],
        out_specs=out_specs,
        out_shape=out_shape,
        scratch_shapes=[pltpu.VMEM((tm, d), BF16),
                        pltpu.VMEM((tm, D_GMLP), F32),
                        pltpu.VMEM((N_SGU_GROUPS, mix_rows, mix_rows), BF16)],
        compiler_params=_params(("arbitrary", "arbitrary")),
        name="gmlp",
    )(x, g.reshape(1, d), w_in, ln_g.reshape(1, D_GMLP), ln_b.reshape(1, D_GMLP),
      ws_tiled, bs_tiled, w_out)


def _top_blocks(gate, blk, n_take, axis):
    nb = gate.shape[axis]
    picks = []
    sel = jnp.zeros_like(gate)
    for r in range(MOBA_TOP_K):
        m = jnp.max(gate, axis=axis, keepdims=True)
        idx = jnp.min(jnp.where(gate == m, blk, float(nb)), axis=axis, keepdims=True)
        hit = blk == idx
        picks.append(idx)
        counts = 1.0 if n_take is None else jnp.where(r < n_take, 1.0, 0.0)
        sel = jnp.maximum(sel, jnp.where(hit, counts, 0.0))
        gate = jnp.where(hit, NEG_INF, gate)
    return picks, sel


def _kv_proj_kernel(x_ref, g_ref, wk_ref, wv_ref, k_ref, v_ref, kb_ref, vb_ref, kmean_ref, xn_ref):
    @pl.when(pl.program_id(1) == 0)
    def _():
        xn_ref[...] = _rmsnorm_rows(x_ref[...], g_ref[...]).astype(BF16)

    xn = xn_ref[...]
    k = _dot(xn, wk_ref[...])
    v = _dot(xn, wv_ref[...])
    k_ref[...] = k
    v_ref[...] = v
    kb_ref[...] = k.astype(BF16)
    vb_ref[...] = v.astype(BF16)
    for blk in range(kmean_ref.shape[0]):
        rows = slice(blk * MOBA_BLOCK, (blk + 1) * MOBA_BLOCK)
        kmean_ref[blk] = jnp.mean(k[rows, :], axis=0, keepdims=True)


def _kv_proj(x, g, wk, wv, *, tm, tn):
    rows, d = x.shape
    n = wk.shape[1]
    blocks_per_tile = tm // MOBA_BLOCK
    tile = lambda: pl.BlockSpec((tm, tn), lambda i, j: (i, j))
    return pl.pallas_call(
        _kv_proj_kernel,
        grid=(rows // tm, n // tn),
        in_specs=[pl.BlockSpec((tm, d), lambda i, j: (i, 0)),
                  pl.BlockSpec((1, d), lambda i, j: (0, 0)),
                  _weight_spec(wk, 0, (d, tn), lambda i, j: (0, j)),
                  _weight_spec(wv, 0, (d, tn), lambda i, j: (0, j))],
        out_specs=[tile(), tile(), tile(), tile(),
                   pl.BlockSpec((blocks_per_tile, 1, tn), lambda i, j: (i, 0, j))],
        out_shape=[jax.ShapeDtypeStruct((rows, n), F32), jax.ShapeDtypeStruct((rows, n), F32),
                   jax.ShapeDtypeStruct((rows, n), BF16), jax.ShapeDtypeStruct((rows, n), BF16),
                   jax.ShapeDtypeStruct((rows // MOBA_BLOCK, 1, n), F32)],
        scratch_shapes=[pltpu.VMEM((tm, d), BF16)],
        compiler_params=_params(("parallel", "arbitrary")),
        name="kv_proj",
    )(x, g.reshape(1, d), wk, wv)


HEADS_PER_STEP = 8
BLOCKS_PER_TRIP = 4


def _moba_prompt_kernel(slopes_ref, q_ref, kb_ref, vb_ref, kmean_ref, o_ref,
                        vt_ref, qaug_ref, kaug_ref, sel_ref, acc_ref):
    hg = pl.program_id(1)
    qt = pl.program_id(2)
    n_blocks = kb_ref.shape[1] // MOBA_BLOCK
    score_scale = (HEAD_DIM ** -0.5) * LOG2E
    heads = range(HEADS_PER_STEP)
    cols = [slice(hh * HEAD_DIM, (hh + 1) * HEAD_DIM) for hh in heads]
    slope2 = [slopes_ref[hg * HEADS_PER_STEP + hh] * LOG2E for hh in heads]

    @pl.when(qt == 0)
    def _():
        for j in range(n_blocks):
            rows = slice(j * MOBA_BLOCK, (j + 1) * MOBA_BLOCK)
            vt_ref[:, rows] = vb_ref[0, rows, :].astype(F32).T.astype(BF16)
        key_i = lax.broadcasted_iota(jnp.int32, (MOBA_BLOCK, HEAD_DIM), 0)
        piece = lax.broadcasted_iota(jnp.int32, (MOBA_BLOCK, HEAD_DIM), 1)
        kaug_ref[...] = jnp.where(piece < 3, key_i, 0).astype(BF16)
        part = lax.broadcasted_iota(jnp.int32, (HEAD_DIM, MOBA_BLOCK), 0)
        for hh in heads:
            whole = jnp.full((HEAD_DIM, MOBA_BLOCK), slope2[hh], F32)
            hi = whole.astype(BF16).astype(F32)
            mid = (whole - hi).astype(BF16).astype(F32)
            lo = whole - hi - mid
            pieces = jnp.where(part == 0, hi, jnp.where(part == 1, mid, jnp.where(part == 2, lo, 0.0)))
            qaug_ref[hh, HEAD_DIM:, :] = pieces.astype(BF16)

    key_i = lax.broadcasted_iota(jnp.int32, (MOBA_BLOCK, MOBA_BLOCK), 0)
    qry_i = lax.broadcasted_iota(jnp.int32, (MOBA_BLOCK, MOBA_BLOCK), 1)
    causal = qry_i >= key_i
    own = pl.ds(pl.multiple_of(qt * MOBA_BLOCK, MOBA_BLOCK), MOBA_BLOCK)
    q_ts = [q_ref[0, :, cols[hh]].T for hh in heads]
    for hh in heads:
        qaug_ref[hh, :HEAD_DIM, :] = (q_ts[hh] * score_scale).astype(BF16)
    kaug = kaug_ref[...]

    def scores(rows, hh):
        keys = jnp.concatenate([kb_ref[0, rows, cols[hh]], kaug], axis=1)
        return _dot(keys, qaug_ref[hh])

    raw = [scores(own, hh) for hh in heads]
    gates = [jnp.dot(kmean_ref[0, :, cols[hh]], q_ts[hh], precision=lax.Precision.HIGHEST,
                     preferred_element_type=F32) for hh in heads]
    stats = []
    for hh in heads:
        blk = lax.broadcasted_iota(jnp.int32, gates[hh].shape, 0)
        gate = jnp.where(blk < qt, gates[hh], NEG_INF)
        _, sel = _top_blocks(gate, blk.astype(F32), qt, axis=0)
        sel_ref[hh] = sel
        t = jnp.where(causal, raw[hh], NEG_INF)
        m0 = jnp.max(t, axis=0, keepdims=True)
        p = jnp.exp2(t - m0)
        l0 = jnp.sum(p, axis=0, keepdims=True)
        acc_ref[hh] = _dot(vt_ref[cols[hh], own], p.astype(BF16))
        stats.append((m0, l0))

    def past_blocks(j0, n_now, stats):
        blocks = range(n_now)
        rows = [pl.ds(pl.multiple_of((j0 + i) * MOBA_BLOCK, MOBA_BLOCK), MOBA_BLOCK) for i in blocks]
        raw = [[scores(rows[i], hh) for hh in heads] for i in blocks]
        stats = list(stats)
        for i in blocks:
            blocks_between = ((qt - j0 - i) * MOBA_BLOCK).astype(F32)
            for hh in heads:
                m, l = stats[hh]
                t = raw[i][hh]
                far = slope2[hh] * blocks_between
                picked = sel_ref[hh, pl.ds(j0 + i, 1), :] > 0.0
                m_blk = jnp.max(t, axis=0, keepdims=True) - far
                m_new = jnp.where(picked, jnp.maximum(m, m_blk), m)
                shift = jnp.where(picked, m_new + far, jnp.inf)
                p = jnp.exp2(t - shift)
                alpha = jnp.exp2(m - m_new)
                l = alpha * l + jnp.sum(p, axis=0, keepdims=True)
                acc_ref[hh] = alpha * acc_ref[hh] + _dot(vt_ref[cols[hh], rows[i]], p.astype(BF16))
                stats[hh] = (m_new, l)
        return tuple(stats)

    stats = lax.fori_loop(0, qt // BLOCKS_PER_TRIP,
                          lambda i, st: past_blocks(i * BLOCKS_PER_TRIP, BLOCKS_PER_TRIP, st),
                          tuple(stats))
    done = (qt // BLOCKS_PER_TRIP) * BLOCKS_PER_TRIP
    stats = lax.fori_loop(done, qt, lambda j, st: past_blocks(j, 1, st), stats)
    for hh in heads:
        o_ref[0, :, cols[hh]] = (acc_ref[hh] / stats[hh][1]).T.astype(o_ref.dtype)


def _moba_prompt(q, kb, vb, kmean, slopes):
    bsz, seq, _ = q.shape
    n_qt = seq // MOBA_BLOCK
    width = HEADS_PER_STEP * HEAD_DIM
    return pl.pallas_call(
        _moba_prompt_kernel,
        grid_spec=pltpu.PrefetchScalarGridSpec(
            num_scalar_prefetch=0,
            grid=(bsz, N_HEADS // HEADS_PER_STEP, n_qt),
            in_specs=[pl.BlockSpec(memory_space=pltpu.SMEM),
                      pl.BlockSpec((1, MOBA_BLOCK, width), lambda b, h, t: (b, t, h)),
                      pl.BlockSpec((1, seq, width), lambda b, h, t: (b, 0, h),
                                   pipeline_mode=pl.Buffered(1)),
                      pl.BlockSpec((1, seq, width), lambda b, h, t: (b, 0, h),
                                   pipeline_mode=pl.Buffered(1)),
                      pl.BlockSpec((1, n_qt, width), lambda b, h, t: (b, 0, h))],
            out_specs=pl.BlockSpec((1, MOBA_BLOCK, width), lambda b, h, t: (b, t, h)),
            scratch_shapes=[pltpu.VMEM((width, seq), BF16),
                            pltpu.VMEM((HEADS_PER_STEP, 2 * HEAD_DIM, MOBA_BLOCK), BF16),
                            pltpu.VMEM((MOBA_BLOCK, HEAD_DIM), BF16),
                            pltpu.VMEM((HEADS_PER_STEP, n_qt, MOBA_BLOCK), F32),
                            pltpu.VMEM((HEADS_PER_STEP, HEAD_DIM, MOBA_BLOCK), F32)]),
        out_shape=jax.ShapeDtypeStruct(q.shape, BF16),
        compiler_params=_params(("parallel", "parallel", "arbitrary")),
        name="moba_prompt",
    )(slopes, q, kb, vb, kmean)


MEANS_BLOCKS_PER_STEP = 4


def _cached_means_kernel(pt_ref, *refs, ppb):
    del pt_ref
    o_ref = refs[-1]
    page_refs = refs[:-1]
    for blk in range(MEANS_BLOCKS_PER_STEP):
        total = jnp.sum(page_refs[blk * ppb][0], axis=0)
        for p in range(1, ppb):
            total = total + jnp.sum(page_refs[blk * ppb + p][0], axis=0)
        o_ref[0, blk] = total / MOBA_BLOCK


def _cached_means(cache_k, pt_flat, db, n_pages):
    _, page, n_kv, dh = cache_k.shape
    ppb = MOBA_BLOCK // page
    nfb = (n_pages * page) // MOBA_BLOCK
    pages_per_step = MEANS_BLOCKS_PER_STEP * ppb
    assert nfb % MEANS_BLOCKS_PER_STEP == 0

    def page_spec(p):
        return pl.BlockSpec((1, page, n_kv, dh),
                            lambda b, n, pt: (pt[b * n_pages + n * pages_per_step + p], 0, 0, 0))

    return pl.pallas_call(
        functools.partial(_cached_means_kernel, ppb=ppb),
        grid_spec=pltpu.PrefetchScalarGridSpec(
            num_scalar_prefetch=1,
            grid=(db, nfb // MEANS_BLOCKS_PER_STEP),
            in_specs=[page_spec(p) for p in range(pages_per_step)],
            out_specs=pl.BlockSpec((1, MEANS_BLOCKS_PER_STEP, n_kv, dh),
                                   lambda b, n, pt: (b, n, 0, 0))),
        out_shape=jax.ShapeDtypeStruct((db, nfb, n_kv, dh), F32),
        compiler_params=_params(("parallel", "parallel")),
        name="cached_means",
    )(pt_flat, *([cache_k] * pages_per_step))


def _sample_topk_kernel(q_ref, km_ref, o_ref):
    t = q_ref.shape[1]
    lane = lax.broadcasted_iota(jnp.int32, (t, 128), 1)
    for h in range(N_HEADS):
        cols = slice(h * HEAD_DIM, (h + 1) * HEAD_DIM)
        gate = _dot_nt(q_ref[0, :, cols], km_ref[0, h], precision=lax.Precision.HIGHEST)
        blk_f = lax.broadcasted_iota(jnp.int32, gate.shape, 1).astype(F32)
        picks, _ = _top_blocks(gate, blk_f, None, axis=1)
        out = jnp.zeros((t, 128), F32)
        for r, idx in enumerate(picks):
            out = jnp.where(lane == r, idx, out)
        o_ref[0, h] = out.astype(jnp.int32)


def _sample_topk(q, kmean):
    db, t, _ = q.shape
    nfb = kmean.shape[2]
    assert nfb >= MOBA_TOP_K
    out = pl.pallas_call(
        _sample_topk_kernel,
        grid=(db,),
        in_specs=[pl.BlockSpec((1, t, D_MODEL), lambda b: (b, 0, 0)),
                  pl.BlockSpec((1, N_HEADS, nfb, HEAD_DIM), lambda b: (b, 0, 0, 0))],
        out_specs=pl.BlockSpec((1, N_HEADS, t, 128), lambda b: (b, 0, 0, 0)),
        out_shape=jax.ShapeDtypeStruct((db, N_HEADS, t, 128), jnp.int32),
        compiler_params=_params(("parallel",)),
        name="sample_topk",
    )(q, kmean)
    return out[..., :MOBA_TOP_K]


def _sample_pages_fetch(step, n_steps, idx_ref, pt_ref, ck_hbm, cv_hbm, kbuf, vbuf, sems,
                        *, n_t, n_sel_pages, ppb, n_pages):
    slot = step % 2

    def page_copies(at_step, sl):
        bb = at_step // N_HEADS
        hh = at_step % N_HEADS
        copies = []
        for t in range(n_t):
            for n in range(n_sel_pages):
                block = idx_ref[(at_step * n_t + t) * MOBA_TOP_K + n // ppb]
                phys = pt_ref[bb * n_pages + block * ppb + n % ppb]
                dst = t * n_sel_pages + n
                copies.append(pltpu.make_async_copy(ck_hbm.at[phys, :, hh, :], kbuf.at[sl, dst],
                                                    sems.at[0, sl]))
                copies.append(pltpu.make_async_copy(cv_hbm.at[phys, :, hh, :], vbuf.at[sl, dst],
                                                    sems.at[1, sl]))
        return copies

    @pl.when(step == 0)
    def _():
        for cp in page_copies(step, slot):
            cp.start()

    @pl.when(step + 1 < n_steps)
    def _():
        for cp in page_copies(step + 1, 1 - slot):
            cp.start()

    for cp in page_copies(step, slot):
        cp.wait()


def _sample_attention_step(step, idx_ref, slope, q_ref, kn_ref, vn_ref, o_ref, kbuf, vbuf,
                           *, n_sel_pages, ppb, n_pages):
    n_t = q_ref.shape[1]
    page = kbuf.shape[2]
    past_len = n_pages * page
    slot = step % 2
    scale = HEAD_DIM ** -0.5
    off_bias = slope * lax.broadcasted_iota(jnp.int32, (page, 1), 0).astype(F32)
    new_i = lax.broadcasted_iota(jnp.int32, (n_t, 1), 0)
    for t in range(n_t):
        q = q_ref[0, t:t + 1, :]
        pos_q = past_len + t
        scores = []
        for n in range(n_sel_pages):
            s = jnp.sum(kbuf[slot, t * n_sel_pages + n] * q, axis=1, keepdims=True) * scale
            block = idx_ref[(step * n_t + t) * MOBA_TOP_K + n // ppb]
            to_page = (pos_q - (block * ppb + n % ppb) * page).astype(F32)
            scores.append(s + off_bias - slope * to_page)
        s_new = jnp.sum(kn_ref[0] * q, axis=1, keepdims=True) * scale
        d_new = t - new_i
        s_new = jnp.where(d_new >= 0, s_new - slope * d_new.astype(F32), NEG_INF)

        m = jnp.max(s_new, axis=0, keepdims=True)
        for s in scores:
            m = jnp.maximum(m, jnp.max(s, axis=0, keepdims=True))
        p_new = jnp.exp(s_new - m)
        l = jnp.sum(p_new, axis=0, keepdims=True)
        acc = jnp.sum(p_new * vn_ref[0], axis=0, keepdims=True)
        for n, s in enumerate(scores):
            p = jnp.exp(s - m)
            l = l + jnp.sum(p, axis=0, keepdims=True)
            acc = acc + jnp.sum(p * vbuf[slot, t * n_sel_pages + n], axis=0, keepdims=True)
        o_ref[0, t:t + 1, :] = acc / l


def _ffn_hosting_attention_kernel(idx_ref, pt_ref, slopes_ref, x_ref, g_ref, w1_ref, w2_ref,
                                  q_ref, kn_ref, vn_ref, ck_hbm, cv_hbm, o_ref, att_ref,
                                  xn_ref, kbuf, vbuf, sems, *, n_sel_pages, ppb, n_pages):
    i = pl.program_id(0)
    j = pl.program_id(1)
    step = i * pl.num_programs(1) + j
    n_steps = pl.num_programs(0) * pl.num_programs(1)
    _sample_pages_fetch(step, n_steps, idx_ref, pt_ref, ck_hbm, cv_hbm, kbuf, vbuf, sems,
                        n_t=q_ref.shape[1], n_sel_pages=n_sel_pages, ppb=ppb, n_pages=n_pages)

    @pl.when(j == 0)
    def _():
        x = x_ref[...]
        xn_ref[...] = _rmsnorm_rows(x, g_ref[...]).astype(BF16)
        o_ref[...] = x

    h = jnp.maximum(_dot(xn_ref[...], w1_ref[...]), 0.0)
    o_ref[...] += _dot((h * h).astype(BF16), w2_ref[...])
    _sample_attention_step(step, idx_ref, slopes_ref[step % N_HEADS], q_ref, kn_ref, vn_ref,
                           att_ref, kbuf, vbuf, n_sel_pages=n_sel_pages, ppb=ppb, n_pages=n_pages)


def _ffn_hosting_attention(x, g, w1, w2, q, k_new, v_new, cache_k, cache_v, idx_flat, pt_flat,
                           slopes, n_pages, *, tm, tf):
    rows, d = x.shape
    dff = w1.shape[-1]
    db, t, _ = q.shape
    page = cache_k.shape[1]
    ppb = MOBA_BLOCK // page
    n_sel_pages = MOBA_TOP_K * ppb
    n_j = dff // tf
    assert (rows // tm) * n_j == db * N_HEADS

    def new_spec():
        return pl.BlockSpec((1, t, HEAD_DIM),
                            lambda i, j, idx, pt: ((i * n_j + j) // N_HEADS, 0, (i * n_j + j) % N_HEADS))

    kern = functools.partial(_ffn_hosting_attention_kernel, n_sel_pages=n_sel_pages, ppb=ppb,
                             n_pages=n_pages)
    return pl.pallas_call(
        kern,
        grid_spec=pltpu.PrefetchScalarGridSpec(
            num_scalar_prefetch=2,
            grid=(rows // tm, n_j),
            in_specs=[pl.BlockSpec(memory_space=pltpu.SMEM),
                      pl.BlockSpec((tm, d), lambda i, j, idx, pt: (i, 0)),
                      pl.BlockSpec((1, d), lambda i, j, idx, pt: (0, 0)),
                      pl.BlockSpec((d, tf), lambda i, j, idx, pt: (0, j)),
                      pl.BlockSpec((tf, d), lambda i, j, idx, pt: (j, 0)),
                      new_spec(), new_spec(), new_spec(),
                      pl.BlockSpec(memory_space=pl.ANY), pl.BlockSpec(memory_space=pl.ANY)],
            out_specs=[pl.BlockSpec((tm, d), lambda i, j, idx, pt: (i, 0)), new_spec()],
            scratch_shapes=[pltpu.VMEM((tm, d), BF16),
                            pltpu.VMEM((2, t * n_sel_pages, page, HEAD_DIM), F32),
                            pltpu.VMEM((2, t * n_sel_pages, page, HEAD_DIM), F32),
                            pltpu.SemaphoreType.DMA((2, 2))]),
        out_shape=[jax.ShapeDtypeStruct((rows, d), F32), jax.ShapeDtypeStruct(q.shape, F32)],
        compiler_params=_params(("arbitrary", "arbitrary")),
        name="ffn_hosting_attention",
    )(idx_flat, pt_flat, slopes, x, g.reshape(1, d), w1, w2, q, k_new, v_new, cache_k, cache_v)


def kernel(x_prompt, x_sample, cache_k, cache_v, page_table, norm_mix_g, norm_ffn_g,
           gmlp_w_in, gmlp_ln_g, gmlp_ln_b, gmlp_w_s, gmlp_b_s, gmlp_w_out,
           kv_norm_g, w_k, w_v, attn_w_q, attn_w_o, ffn_w1, ffn_w2, final_norm_g):
    bp, sp, d = x_prompt.shape
    bs, ss, _ = x_sample.shape
    depth = norm_mix_g.shape[0]
    n_a = gmlp_w_in.shape[0]
    _, page, n_kv, dh = cache_k.shape
    n_pages = page_table.shape[1]
    assert sp % MOBA_BLOCK == 0 and sp % CHUNK == 0 and ss <= CHUNK
    assert (n_pages * page) % MOBA_BLOCK == 0
    assert n_kv == N_HEADS and dh == HEAD_DIM

    slopes = jnp.exp2(-8.0 * jnp.arange(1, N_HEADS + 1, dtype=F32) / N_HEADS)

    rows_p, rows_s = bp * sp, bs * ss
    xp = x_prompt.reshape(rows_p, d)
    xs = x_sample.reshape(rows_s, d)
    pt_flat = page_table.reshape(-1)

    tm_p = 512
    assert n_a >= 1
    sgu_p, sgu_s = [], []
    att_s = None
    for layer in range(depth):
        g = norm_mix_g[layer]
        last = layer == depth - 1
        if layer < n_a:
            ws_s = jnp.tile(gmlp_w_s[layer][:, :ss, :ss], (1, bs, bs))
            bs_s = jnp.tile(gmlp_b_s[layer][:, :ss].T, (bs, 1))
            xs, vs, w_in, w_out = _gmlp(
                xs, g, gmlp_w_in, gmlp_ln_g[layer], gmlp_ln_b[layer], ws_s, bs_s, gmlp_w_out,
                tm=rows_s, tn=512, chunk_len=ss, v_period=1, layer=layer, emit_bf16=True)
            sgu_s.append(vs.reshape(bs, ss, D_GMLP))
        else:
            a = layer - n_a
            xs, wo_b = _proj_residual(att_s.reshape(rows_s, d), attn_w_o, xs, tm=rows_s, tn=512,
                                      layer=a, emit_bf16=True)
        xs, w1_b, w2_b = _ffn(xs, norm_ffn_g[layer], ffn_w1, ffn_w2, final_norm_g, tm=rows_s,
                              tf=512, final_norm=last, layer=layer, emit_bf16=True)

        ahead = None
        if layer + 1 >= n_a and not last:
            if layer + 1 == n_a:
                ks_, vs_, wk_b, wv_b = _norm_matmul(xs, kv_norm_g, [w_k, w_v], tm=rows_s, tn=512,
                                                   emit_bf16=True)
                kmean_s = _cached_means(cache_k, pt_flat, bs, n_pages).transpose(0, 2, 1, 3)
            qs, wq_next = _norm_matmul(xs, norm_mix_g[layer + 1], [attn_w_q], tm=rows_s, tn=512,
                                       layer=layer + 1 - n_a, emit_bf16=True)
            qs3 = qs.reshape(bs, ss, d)
            ahead = (qs3, _sample_topk(qs3, kmean_s).reshape(-1))

        if layer < n_a:
            xp, vp = _gmlp(xp, g, w_in, gmlp_ln_g[layer], gmlp_ln_b[layer], gmlp_w_s[layer],
                           gmlp_b_s[layer].T, w_out, tm=tm_p, tn=1024, chunk_len=CHUNK,
                           v_period=sp // tm_p)
            sgu_p.append(vp.reshape(bp, CHUNK, D_GMLP))
        else:
            if layer == n_a:
                kp, vp_, kb_p, vb_p, kmean_p = _kv_proj(xp, kv_norm_g, wk_b, wv_b, tm=256, tn=d)
            (qp,) = _norm_matmul(xp, g, [wq_b], tm=tm_p, tn=d)
            att_p = _moba_prompt(qp.reshape(bp, sp, d), kb_p.reshape(bp, sp, d),
                                 vb_p.reshape(bp, sp, d),
                                 kmean_p.reshape(bp, sp // MOBA_BLOCK, d), slopes)
            (xp,) = _proj_residual(att_p.reshape(rows_p, d), wo_b, xp, tm=tm_p, tn=d)
        if ahead is None:
            (xp,) = _ffn(xp, norm_ffn_g[layer], w1_b, w2_b, final_norm_g, tm=tm_p, tf=1024,
                         final_norm=last)
        else:
            xp, att_s = _ffn_hosting_attention(
                xp, norm_ffn_g[layer], w1_b, w2_b, ahead[0], ks_.reshape(bs, ss, d),
                vs_.reshape(bs, ss, d), cache_k, cache_v, ahead[1], pt_flat, slopes, n_pages,
                tm=tm_p, tf=1024)
            wq_b = wq_next

    kv_shape_p = (bp, sp, n_kv, dh)
    kv_shape_s = (bs, ss, n_kv, dh)
    return (xp.reshape(bp, sp, d), xs.reshape(bs, ss, d),
            kp.reshape(kv_shape_p), vp_.reshape(kv_shape_p),
            ks_.reshape(kv_shape_s), vs_.reshape(kv_shape_s),
            jnp.stack(sgu_p), jnp.stack(sgu_s))
```

```python
import functools

import jax
import jax.numpy as jnp
from jax import lax
from jax.experimental import pallas as pl
from jax.experimental.pallas import tpu as pltpu

D_MODEL = 2048
N_HEADS = 16
HEAD_DIM = 128
MOBA_BLOCK = 256
MOBA_TOP_K = 3
CHUNK = 128
D_GMLP = 2 * D_MODEL
N_SGU_GROUPS = 16
SGU_GROUP = D_GMLP // N_SGU_GROUPS
D_FF = 4 * D_MODEL
NORM_EPS = 1e-6

V7X_VMEM_LIMIT_BYTES = 56 * 1024 * 1024

F32 = jnp.float32
BF16 = jnp.bfloat16
NEG_INF = float("-inf")
LOG2E = 1.4426950408889634


def _params(semantics):
    return pltpu.CompilerParams(dimension_semantics=semantics,
                                vmem_limit_bytes=V7X_VMEM_LIMIT_BYTES)


def _rmsnorm_rows(x, g):
    return x * lax.rsqrt(jnp.mean(x * x, axis=-1, keepdims=True) + NORM_EPS) * g


def _dot(a, b):
    return jnp.dot(a, b, preferred_element_type=F32)


def _dot_nt(a, b, precision=None):
    return lax.dot_general(a, b, (((1,), (1,)), ((), ())), precision=precision,
                           preferred_element_type=F32)


def _weight_spec(w, layer, block, index_map):
    mode = {"pipeline_mode": pl.Buffered(1)} if tuple(block) == tuple(w.shape[-2:]) else {}
    if w.ndim == 2:
        return pl.BlockSpec(block, index_map, **mode)
    return pl.BlockSpec((None,) + block, lambda *args: (layer,) + tuple(index_map(*args)), **mode)


def _use_weight(w_ref, wb_ref):
    w = w_ref[...].astype(BF16)
    if wb_ref is not None:
        wb_ref[...] = w
    return w


def _norm_matmul_kernel(*refs, n_w, emit_bf16):
    x_ref, g_ref = refs[:2]
    w_refs = refs[2:2 + n_w]
    o_refs = refs[2 + n_w:2 + 2 * n_w]
    wb_refs = refs[2 + 2 * n_w:2 + 3 * n_w] if emit_bf16 else [None] * n_w
    xn_ref = refs[-1]

    @pl.when(pl.program_id(1) == 0)
    def _():
        xn_ref[...] = _rmsnorm_rows(x_ref[...], g_ref[...]).astype(BF16)

    xn = xn_ref[...]
    for w_ref, o_ref, wb_ref in zip(w_refs, o_refs, wb_refs):
        o_ref[...] = _dot(xn, _use_weight(w_ref, wb_ref))


def _norm_matmul(x, g, ws, *, tm, tn, layer=0, emit_bf16=False):
    rows, d = x.shape
    n = ws[0].shape[-1]
    n_w = len(ws)
    out_specs = [pl.BlockSpec((tm, tn), lambda i, j: (i, j)) for _ in ws]
    out_shape = [jax.ShapeDtypeStruct((rows, n), F32) for _ in ws]
    if emit_bf16:
        assert rows == tm
        out_specs += [pl.BlockSpec((d, tn), lambda i, j: (0, j)) for _ in ws]
        out_shape += [jax.ShapeDtypeStruct((d, n), BF16) for _ in ws]
    return pl.pallas_call(
        functools.partial(_norm_matmul_kernel, n_w=n_w, emit_bf16=emit_bf16),
        grid=(rows // tm, n // tn),
        in_specs=[pl.BlockSpec((tm, d), lambda i, j: (i, 0)),
                  pl.BlockSpec((1, d), lambda i, j: (0, 0))]
                 + [_weight_spec(w, layer, (d, tn), lambda i, j: (0, j)) for w in ws],
        out_specs=out_specs,
        out_shape=out_shape,
        scratch_shapes=[pltpu.VMEM((tm, d), BF16)],
        compiler_params=_params(("parallel", "arbitrary")),
        name="norm_matmul",
    )(x, g.reshape(1, d), *ws)


def _proj_residual_kernel(a_ref, w_ref, x_ref, o_ref, wb_ref=None):
    o_ref[...] = x_ref[...] + _dot(a_ref[...].astype(BF16), _use_weight(w_ref, wb_ref))


def _proj_residual(a, w, x, *, tm, tn, layer=0, emit_bf16=False):
    rows, k = a.shape
    n = w.shape[-1]
    out_specs = [pl.BlockSpec((tm, tn), lambda i, j: (i, j))]
    out_shape = [jax.ShapeDtypeStruct((rows, n), F32)]
    if emit_bf16:
        assert rows == tm
        out_specs.append(pl.BlockSpec((k, tn), lambda i, j: (0, j)))
        out_shape.append(jax.ShapeDtypeStruct((k, n), BF16))
    return pl.pallas_call(
        _proj_residual_kernel,
        grid=(rows // tm, n // tn),
        in_specs=[pl.BlockSpec((tm, k), lambda i, j: (i, 0)),
                  _weight_spec(w, layer, (k, tn), lambda i, j: (0, j)),
                  pl.BlockSpec((tm, tn), lambda i, j: (i, j))],
        out_specs=out_specs,
        out_shape=out_shape,
        compiler_params=_params(("parallel", "parallel")),
        name="proj_residual",
    )(a, w, x)


def _ffn_step(x_ref, g_ref, w1, w2, o_ref, xn_ref):
    @pl.when(pl.program_id(1) == 0)
    def _():
        x = x_ref[...]
        xn_ref[...] = _rmsnorm_rows(x, g_ref[...]).astype(BF16)
        o_ref[...] = x

    h = jnp.maximum(_dot(xn_ref[...], w1), 0.0)
    o_ref[...] += _dot((h * h).astype(BF16), w2)


def _ffn_kernel(x_ref, g_ref, w1_ref, w2_ref, gf_ref, o_ref, *rest, final_norm, emit_bf16):
    w1b_ref, w2b_ref = rest[:2] if emit_bf16 else (None, None)
    xn_ref = rest[-1]
    j = pl.program_id(1)
    _ffn_step(x_ref, g_ref, _use_weight(w1_ref, w1b_ref), _use_weight(w2_ref, w2b_ref),
              o_ref, xn_ref)

    if final_norm:
        @pl.when(j == pl.num_programs(1) - 1)
        def _():
            o_ref[...] = _rmsnorm_rows(o_ref[...], gf_ref[...])


def _ffn(x, g, w1, w2, gf, *, tm, tf, final_norm, layer=0, emit_bf16=False):
    rows, d = x.shape
    dff = w1.shape[-1]
    out_specs = [pl.BlockSpec((tm, d), lambda i, j: (i, 0))]
    out_shape = [jax.ShapeDtypeStruct((rows, d), F32)]
    if emit_bf16:
        assert rows == tm
        out_specs += [pl.BlockSpec((d, tf), lambda i, j: (0, j)),
                      pl.BlockSpec((tf, d), lambda i, j: (j, 0))]
        out_shape += [jax.ShapeDtypeStruct((d, dff), BF16), jax.ShapeDtypeStruct((dff, d), BF16)]
    return pl.pallas_call(
        functools.partial(_ffn_kernel, final_norm=final_norm, emit_bf16=emit_bf16),
        grid=(rows // tm, dff // tf),
        in_specs=[pl.BlockSpec((tm, d), lambda i, j: (i, 0)),
                  pl.BlockSpec((1, d), lambda i, j: (0, 0)),
                  _weight_spec(w1, layer, (d, tf), lambda i, j: (0, j)),
                  _weight_spec(w2, layer, (tf, d), lambda i, j: (j, 0)),
                  pl.BlockSpec((1, d), lambda i, j: (0, 0))],
        out_specs=out_specs,
        out_shape=out_shape,
        scratch_shapes=[pltpu.VMEM((tm, d), BF16)],
        compiler_params=_params(("parallel", "arbitrary")),
        name="ffn",
    )(x, g.reshape(1, d), w1, w2, gf.reshape(1, d))


def _gmlp_kernel(x_ref, g_ref, win_ref, lng_ref, lnb_ref, ws_ref, bs_ref, wout_ref,
                 y_ref, vout_ref, *rest, nv, tn, chunk_len, mix_rows, emit_bf16):
    winb_ref, woutb_ref = rest[:2] if emit_bf16 else (None, None)
    xn_ref, v_ref, wsb_ref = rest[-3:]
    s = pl.program_id(1)
    tm = x_ref.shape[0]

    @pl.when(s == 0)
    def _():
        x = x_ref[...]
        xn_ref[...] = _rmsnorm_rows(x, g_ref[...]).astype(BF16)
        y_ref[...] = x

    @pl.when(s < nv)
    def _():
        z = jax.nn.gelu(_dot(xn_ref[...], _use_weight(win_ref, winb_ref)))
        v_ref[:, pl.ds(pl.multiple_of(s * tn, tn), tn)] = z

    @pl.when(s == nv)
    def _():
        r = lax.broadcasted_iota(jnp.int32, (mix_rows, mix_rows), 0)
        c = lax.broadcasted_iota(jnp.int32, (mix_rows, mix_rows), 1)
        keep = (r // chunk_len == c // chunk_len) & (c <= r)
        for grp in range(N_SGU_GROUPS):
            wsb_ref[grp] = jnp.where(keep, ws_ref[grp], 0.0).astype(BF16)
        ln_g = lng_ref[...]
        ln_b = lnb_ref[...]

        def mix_tile(t, carry):
            rows = pl.ds(pl.multiple_of(t * mix_rows, mix_rows), mix_rows)
            v = v_ref[rows, :]
            mu = jnp.mean(v, axis=-1, keepdims=True)
            vc = v - mu
            var = jnp.mean(vc * vc, axis=-1, keepdims=True)
            vn = vc * lax.rsqrt(var + NORM_EPS) * ln_g + ln_b
            vout_ref[...] = vn
            for grp in range(N_SGU_GROUPS):
                cols = slice(grp * SGU_GROUP, (grp + 1) * SGU_GROUP)
                v_ref[rows, cols] = (_dot(wsb_ref[grp], vn[:, cols].astype(BF16))
                                     + bs_ref[:, grp:grp + 1])
            return carry

        lax.fori_loop(0, tm // mix_rows, mix_tile, 0)

    @pl.when(s >= nv)
    def _():
        cols = pl.ds(pl.multiple_of((s - nv) * tn, tn), tn)
        u = jax.nn.gelu(_dot(xn_ref[...], _use_weight(win_ref, winb_ref)))
        h = (u * v_ref[:, cols]).astype(BF16)
        y_ref[...] += _dot(h, _use_weight(wout_ref, woutb_ref))


def _gmlp(x, g, w_in, ln_g, ln_b, ws_tiled, bs_tiled, w_out, *, tm, tn, chunk_len,
          v_period, layer=0, emit_bf16=False):
    rows, d = x.shape
    mix_rows = ws_tiled.shape[1]
    nv = D_GMLP // tn
    n_vblocks = rows // (tm * v_period)
    win_tile = lambda i, s: (0, jnp.where(s < nv, s + nv, s - nv))
    wout_tile = lambda i, s: (jnp.maximum(s - nv, 0), 0)
    out_specs = [pl.BlockSpec((tm, d), lambda i, s: (i, 0)),
                 pl.BlockSpec((mix_rows, D_GMLP), lambda i, s: (i // v_period, 0))]
    out_shape = [jax.ShapeDtypeStruct((rows, d), F32),
                 jax.ShapeDtypeStruct((n_vblocks * mix_rows, D_GMLP), F32)]
    if emit_bf16:
        assert rows == tm
        out_specs += [pl.BlockSpec((d, tn), win_tile), pl.BlockSpec((tn, d), wout_tile)]
        out_shape += [jax.ShapeDtypeStruct((d, 2 * D_GMLP), BF16),
                      jax.ShapeDtypeStruct((D_GMLP, d), BF16)]
    kern = functools.partial(_gmlp_kernel, nv=nv, tn=tn, chunk_len=chunk_len,
                             mix_rows=mix_rows, emit_bf16=emit_bf16)
    return pl.pallas_call(
        kern,
        grid=(rows // tm, 2 * nv),
        in_specs=[pl.BlockSpec((tm, d), lambda i, s: (i, 0)),
                  pl.BlockSpec((1, d), lambda i, s: (0, 0)),
                  _weight_spec(w_in, layer, (d, tn), win_tile),
                  pl.BlockSpec((1, D_GMLP), lambda i, s: (0, 0)),
                  pl.BlockSpec((1, D_GMLP), lambda i, s: (0, 0)),
                  pl.BlockSpec((N_SGU_GROUPS, mix_rows, mix_rows), lambda i, s: (0, 0, 0)),
                  pl.BlockSpec((mix_rows, N_SGU_GROUPS), lambda i, s: (0, 0)),
                  _weight_spec(w_out, layer, (tn, d), wout_tile)],
        out_specs=out_specs,
        out_shape=out_shape,
        scratch_shapes=[pltpu.VMEM((tm, d), BF16),
                        pltpu.VMEM((tm, D_GMLP), F32),
                        pltpu.VMEM((N_SGU_GROUPS, mix_rows, mix_rows), BF16)],
        compiler_params=_params(("arbitrary", "arbitrary")),
        name="gmlp",
    )(x, g.reshape(1, d), w_in, ln_g.reshape(1, D_GMLP), ln_b.reshape(1, D_GMLP),
      ws_tiled, bs_tiled, w_out)


def _top_blocks(gate, blk, n_take, axis):
    nb = gate.shape[axis]
    picks = []
    sel = jnp.zeros_like(gate)
    for r in range(MOBA_TOP_K):
        m = jnp.max(gate, axis=axis, keepdims=True)
        idx = jnp.min(jnp.where(gate == m, blk, float(nb)), axis=axis, keepdims=True)
        hit = blk == idx
        picks.append(idx)
        counts = 1.0 if n_take is None else jnp.where(r < n_take, 1.0, 0.0)
        sel = jnp.maximum(sel, jnp.where(hit, counts, 0.0))
        gate = jnp.where(hit, NEG_INF, gate)
    return picks, sel


def _kv_proj_kernel(x_ref, g_ref, wk_ref, wv_ref, k_ref, v_ref, kb_ref, vb_ref, kmean_ref, xn_ref):
    @pl.when(pl.program_id(1) == 0)
    def _():
        xn_ref[...] = _rmsnorm_rows(x_ref[...], g_ref[...]).astype(BF16)

    xn = xn_ref[...]
    k = _dot(xn, wk_ref[...])
    v = _dot(xn, wv_ref[...])
    k_ref[...] = k
    v_ref[...] = v
    kb_ref[...] = k.astype(BF16)
    vb_ref[...] = v.astype(BF16)
    for blk in range(kmean_ref.shape[0]):
        rows = slice(blk * MOBA_BLOCK, (blk + 1) * MOBA_BLOCK)
        kmean_ref[blk] = jnp.mean(k[rows, :], axis=0, keepdims=True)


def _kv_proj(x, g, wk, wv, *, tm, tn):
    rows, d = x.shape
    n = wk.shape[1]
    blocks_per_tile = tm // MOBA_BLOCK
    tile = lambda: pl.BlockSpec((tm, tn), lambda i, j: (i, j))
    return pl.pallas_call(
        _kv_proj_kernel,
        grid=(rows // tm, n // tn),
        in_specs=[pl.BlockSpec((tm, d), lambda i, j: (i, 0)),
                  pl.BlockSpec((1, d), lambda i, j: (0, 0)),
                  _weight_spec(wk, 0, (d, tn), lambda i, j: (0, j)),
                  _weight_spec(wv, 0, (d, tn), lambda i, j: (0, j))],
        out_specs=[tile(), tile(), tile(), tile(),
                   pl.BlockSpec((blocks_per_tile, 1, tn), lambda i, j: (i, 0, j))],
        out_shape=[jax.ShapeDtypeStruct((rows, n), F32), jax.ShapeDtypeStruct((rows, n), F32),
                   jax.ShapeDtypeStruct((rows, n), BF16), jax.ShapeDtypeStruct((rows, n), BF16),
                   jax.ShapeDtypeStruct((rows // MOBA_BLOCK, 1, n), F32)],
        scratch_shapes=[pltpu.VMEM((tm, d), BF16)],
        compiler_params=_params(("parallel", "arbitrary")),
        name="kv_proj",
    )(x, g.reshape(1, d), wk, wv)


HEADS_PER_STEP = 8
BLOCKS_PER_TRIP = 4


def _moba_prompt_kernel(slopes_ref, q_ref, kb_ref, vb_ref, kmean_ref, o_ref,
                        vt_ref, qaug_ref, kaug_ref, sel_ref, acc_ref):
    hg = pl.program_id(1)
    qt = pl.program_id(2)
    n_blocks = kb_ref.shape[1] // MOBA_BLOCK
    score_scale = (HEAD_DIM ** -0.5) * LOG2E
    heads = range(HEADS_PER_STEP)
    cols = [slice(hh * HEAD_DIM, (hh + 1) * HEAD_DIM) for hh in heads]
    slope2 = [slopes_ref[hg * HEADS_PER_STEP + hh] * LOG2E for hh in heads]

    @pl.when(qt == 0)
    def _():
        for j in range(n_blocks):
            rows = slice(j * MOBA_BLOCK, (j + 1) * MOBA_BLOCK)
            vt_ref[:, rows] = vb_ref[0, rows, :].astype(F32).T.astype(BF16)
        key_i = lax.broadcasted_iota(jnp.int32, (MOBA_BLOCK, HEAD_DIM), 0)
        piece = lax.broadcasted_iota(jnp.int32, (MOBA_BLOCK, HEAD_DIM), 1)
        kaug_ref[...] = jnp.where(piece < 3, key_i, 0).astype(BF16)
        part = lax.broadcasted_iota(jnp.int32, (HEAD_DIM, MOBA_BLOCK), 0)
        for hh in heads:
            whole = jnp.full((HEAD_DIM, MOBA_BLOCK), slope2[hh], F32)
            hi = whole.astype(BF16).astype(F32)
            mid = (whole - hi).astype(BF16).astype(F32)
            lo = whole - hi - mid
            pieces = jnp.where(part == 0, hi, jnp.where(part == 1, mid, jnp.where(part == 2, lo, 0.0)))
            qaug_ref[hh, HEAD_DIM:, :] = pieces.astype(BF16)

    key_i = lax.broadcasted_iota(jnp.int32, (MOBA_BLOCK, MOBA_BLOCK), 0)
    qry_i = lax.broadcasted_iota(jnp.int32, (MOBA_BLOCK, MOBA_BLOCK), 1)
    causal = qry_i >= key_i
    own = pl.ds(pl.multiple_of(qt * MOBA_BLOCK, MOBA_BLOCK), MOBA_BLOCK)
    q_ts = [q_ref[0, :, cols[hh]].T for hh in heads]
    for hh in heads:
        qaug_ref[hh, :HEAD_DIM, :] = (q_ts[hh] * score_scale).astype(BF16)
    kaug = kaug_ref[...]

    def scores(rows, hh):
        keys = jnp.concatenate([kb_ref[0, rows, cols[hh]], kaug], axis=1)
        return _dot(keys, qaug_ref[hh])

    raw = [scores(own, hh) for hh in heads]
    gates = [jnp.dot(kmean_ref[0, :, cols[hh]], q_ts[hh], precision=lax.Precision.HIGHEST,
                     preferred_element_type=F32) for hh in heads]
    stats = []
    for hh in heads:
        blk = lax.broadcasted_iota(jnp.int32, gates[hh].shape, 0)
        gate = jnp.where(blk < qt, gates[hh], NEG_INF)
        _, sel = _top_blocks(gate, blk.astype(F32), qt, axis=0)
        sel_ref[hh] = sel
        t = jnp.where(causal, raw[hh], NEG_INF)
        m0 = jnp.max(t, axis=0, keepdims=True)
        p = jnp.exp2(t - m0)
        l0 = jnp.sum(p, axis=0, keepdims=True)
        acc_ref[hh] = _dot(vt_ref[cols[hh], own], p.astype(BF16))
        stats.append((m0, l0))

    def past_blocks(j0, n_now, stats):
        blocks = range(n_now)
        rows = [pl.ds(pl.multiple_of((j0 + i) * MOBA_BLOCK, MOBA_BLOCK), MOBA_BLOCK) for i in blocks]
        raw = [[scores(rows[i], hh) for hh in heads] for i in blocks]
        stats = list(stats)
        for i in blocks:
            blocks_between = ((qt - j0 - i) * MOBA_BLOCK).astype(F32)
            for hh in heads:
                m, l = stats[hh]
                t = raw[i][hh]
                far = slope2[hh] * blocks_between
                picked = sel_ref[hh, pl.ds(j0 + i, 1), :] > 0.0
                m_blk = jnp.max(t, axis=0, keepdims=True) - far
                m_new = jnp.where(picked, jnp.maximum(m, m_blk), m)
                shift = jnp.where(picked, m_new + far, jnp.inf)
                p = jnp.exp2(t - shift)
                alpha = jnp.exp2(m - m_new)
                l = alpha * l + jnp.sum(p, axis=0, keepdims=True)
                acc_ref[hh] = alpha * acc_ref[hh] + _dot(vt_ref[cols[hh], rows[i]], p.astype(BF16))
                stats[hh] = (m_new, l)
        return tuple(stats)

    stats = lax.fori_loop(0, qt // BLOCKS_PER_TRIP,
                          lambda i, st: past_blocks(i * BLOCKS_PER_TRIP, BLOCKS_PER_TRIP, st),
                          tuple(stats))
    done = (qt // BLOCKS_PER_TRIP) * BLOCKS_PER_TRIP
    stats = lax.fori_loop(done, qt, lambda j, st: past_blocks(j, 1, st), stats)
    for hh in heads:
        o_ref[0, :, cols[hh]] = (acc_ref[hh] / stats[hh][1]).T.astype(o_ref.dtype)


def _moba_prompt(q, kb, vb, kmean, slopes):
    bsz, seq, _ = q.shape
    n_qt = seq // MOBA_BLOCK
    width = HEADS_PER_STEP * HEAD_DIM
    return pl.pallas_call(
        _moba_prompt_kernel,
        grid_spec=pltpu.PrefetchScalarGridSpec(
            num_scalar_prefetch=0,
            grid=(bsz, N_HEADS // HEADS_PER_STEP, n_qt),
            in_specs=[pl.BlockSpec(memory_space=pltpu.SMEM),
                      pl.BlockSpec((1, MOBA_BLOCK, width), lambda b, h, t: (b, t, h)),
                      pl.BlockSpec((1, seq, width), lambda b, h, t: (b, 0, h),
                                   pipeline_mode=pl.Buffered(1)),
                      pl.BlockSpec((1, seq, width), lambda b, h, t: (b, 0, h),
                                   pipeline_mode=pl.Buffered(1)),
                      pl.BlockSpec((1, n_qt, width), lambda b, h, t: (b, 0, h))],
            out_specs=pl.BlockSpec((1, MOBA_BLOCK, width), lambda b, h, t: (b, t, h)),
            scratch_shapes=[pltpu.VMEM((width, seq), BF16),
                            pltpu.VMEM((HEADS_PER_STEP, 2 * HEAD_DIM, MOBA_BLOCK), BF16),
                            pltpu.VMEM((MOBA_BLOCK, HEAD_DIM), BF16),
                            pltpu.VMEM((HEADS_PER_STEP, n_qt, MOBA_BLOCK), F32),
                            pltpu.VMEM((HEADS_PER_STEP, HEAD_DIM, MOBA_BLOCK), F32)]),
        out_shape=jax.ShapeDtypeStruct(q.shape, BF16),
        compiler_params=_params(("parallel", "parallel", "arbitrary")),
        name="moba_prompt",
    )(slopes, q, kb, vb, kmean)


def _page_block_means(page_refs, o_ref, ppb):
    for blk in range(len(page_refs) // ppb):
        total = jnp.sum(page_refs[blk * ppb][0], axis=0)
        for p in range(1, ppb):
            total = total + jnp.sum(page_refs[blk * ppb + p][0], axis=0)
        o_ref[0, blk] = total / MOBA_BLOCK


def _ffn_hosting_means_kernel(pt_ref, x_ref, g_ref, w1_ref, w2_ref, *rest, n_page_refs, ppb):
    del pt_ref
    page_refs = rest[:n_page_refs]
    o_ref, kmean_ref, xn_ref = rest[n_page_refs:]
    _ffn_step(x_ref, g_ref, w1_ref[...], w2_ref[...], o_ref, xn_ref)
    _page_block_means(page_refs, kmean_ref, ppb)


def _ffn_hosting_means(x, g, w1, w2, cache_k, pt_flat, db, n_pages, *, tm, tf):
    rows, d = x.shape
    dff = w1.shape[-1]
    _, page, n_kv, dh = cache_k.shape
    ppb = MOBA_BLOCK // page
    nfb = (n_pages * page) // MOBA_BLOCK
    n_j = dff // tf
    n_steps = (rows // tm) * n_j
    assert (db * nfb) % n_steps == 0 and nfb % ((db * nfb) // n_steps) == 0
    blocks_per_step = (db * nfb) // n_steps
    pages_per_step = blocks_per_step * ppb
    steps_per_batch = nfb // blocks_per_step

    def page_spec(p):
        def index_map(i, j, pt):
            step = i * n_j + j
            logical = (step % steps_per_batch) * pages_per_step + p
            return (pt[(step // steps_per_batch) * n_pages + logical], 0, 0, 0)
        return pl.BlockSpec((1, page, n_kv, dh), index_map)

    def means_map(i, j, pt):
        step = i * n_j + j
        return (step // steps_per_batch, step % steps_per_batch, 0, 0)

    kern = functools.partial(_ffn_hosting_means_kernel, n_page_refs=pages_per_step, ppb=ppb)
    return pl.pallas_call(
        kern,
        grid_spec=pltpu.PrefetchScalarGridSpec(
            num_scalar_prefetch=1,
            grid=(rows // tm, n_j),
            in_specs=[pl.BlockSpec((tm, d), lambda i, j, pt: (i, 0)),
                      pl.BlockSpec((1, d), lambda i, j, pt: (0, 0)),
                      pl.BlockSpec((d, tf), lambda i, j, pt: (0, j)),
                      pl.BlockSpec((tf, d), lambda i, j, pt: (j, 0))]
                     + [page_spec(p) for p in range(pages_per_step)],
            out_specs=[pl.BlockSpec((tm, d), lambda i, j, pt: (i, 0)),
                       pl.BlockSpec((1, blocks_per_step, n_kv, dh), means_map)],
            scratch_shapes=[pltpu.VMEM((tm, d), BF16)]),
        out_shape=[jax.ShapeDtypeStruct((rows, d), F32),
                   jax.ShapeDtypeStruct((db, nfb, n_kv, dh), F32)],
        compiler_params=_params(("arbitrary", "arbitrary")),
        name="ffn_hosting_means",
    )(pt_flat, x, g.reshape(1, d), w1, w2, *([cache_k] * pages_per_step))


def _sample_topk_kernel(q_ref, km_ref, o_ref):
    t = q_ref.shape[1]
    lane = lax.broadcasted_iota(jnp.int32, (t, 128), 1)
    for h in range(N_HEADS):
        cols = slice(h * HEAD_DIM, (h + 1) * HEAD_DIM)
        gate = _dot_nt(q_ref[0, :, cols], km_ref[0, h], precision=lax.Precision.HIGHEST)
        blk_f = lax.broadcasted_iota(jnp.int32, gate.shape, 1).astype(F32)
        picks, _ = _top_blocks(gate, blk_f, None, axis=1)
        out = jnp.zeros((t, 128), F32)
        for r, idx in enumerate(picks):
            out = jnp.where(lane == r, idx, out)
        o_ref[0, h] = out.astype(jnp.int32)


def _sample_topk(q, kmean):
    db, t, _ = q.shape
    nfb = kmean.shape[2]
    assert nfb >= MOBA_TOP_K
    out = pl.pallas_call(
        _sample_topk_kernel,
        grid=(db,),
        in_specs=[pl.BlockSpec((1, t, D_MODEL), lambda b: (b, 0, 0)),
                  pl.BlockSpec((1, N_HEADS, nfb, HEAD_DIM), lambda b: (b, 0, 0, 0))],
        out_specs=pl.BlockSpec((1, N_HEADS, t, 128), lambda b: (b, 0, 0, 0)),
        out_shape=jax.ShapeDtypeStruct((db, N_HEADS, t, 128), jnp.int32),
        compiler_params=_params(("parallel",)),
        name="sample_topk",
    )(q, kmean)
    return out[..., :MOBA_TOP_K]


def _sample_pages_fetch(step, n_steps, idx_ref, pt_ref, ck_hbm, cv_hbm, kbuf, vbuf, sems,
                        *, n_t, n_sel_pages, ppb, n_pages):
    slot = step % 2

    def page_copies(at_step, sl):
        bb = at_step // N_HEADS
        hh = at_step % N_HEADS
        copies = []
        for t in range(n_t):
            for n in range(n_sel_pages):
                block = idx_ref[(at_step * n_t + t) * MOBA_TOP_K + n // ppb]
                phys = pt_ref[bb * n_pages + block * ppb + n % ppb]
                dst = t * n_sel_pages + n
                copies.append(pltpu.make_async_copy(ck_hbm.at[phys, :, hh, :], kbuf.at[sl, dst],
                                                    sems.at[0, sl]))
                copies.append(pltpu.make_async_copy(cv_hbm.at[phys, :, hh, :], vbuf.at[sl, dst],
                                                    sems.at[1, sl]))
        return copies

    @pl.when(step == 0)
    def _():
        for cp in page_copies(step, slot):
            cp.start()

    @pl.when(step + 1 < n_steps)
    def _():
        for cp in page_copies(step + 1, 1 - slot):
            cp.start()

    for cp in page_copies(step, slot):
        cp.wait()


def _sample_attention_step(step, idx_ref, slope, q_ref, kn_ref, vn_ref, o_ref, kbuf, vbuf,
                           *, n_sel_pages, ppb, n_pages):
    n_t = q_ref.shape[1]
    page = kbuf.shape[2]
    past_len = n_pages * page
    slot = step % 2
    scale = HEAD_DIM ** -0.5
    off_bias = slope * lax.broadcasted_iota(jnp.int32, (page, 1), 0).astype(F32)
    new_i = lax.broadcasted_iota(jnp.int32, (n_t, 1), 0)
    for t in range(n_t):
        q = q_ref[0, t:t + 1, :]
        pos_q = past_len + t
        scores = []
        for n in range(n_sel_pages):
            s = jnp.sum(kbuf[slot, t * n_sel_pages + n] * q, axis=1, keepdims=True) * scale
            block = idx_ref[(step * n_t + t) * MOBA_TOP_K + n // ppb]
            to_page = (pos_q - (block * ppb + n % ppb) * page).astype(F32)
            scores.append(s + off_bias - slope * to_page)
        s_new = jnp.sum(kn_ref[0] * q, axis=1, keepdims=True) * scale
        d_new = t - new_i
        s_new = jnp.where(d_new >= 0, s_new - slope * d_new.astype(F32), NEG_INF)

        m = jnp.max(s_new, axis=0, keepdims=True)
        for s in scores:
            m = jnp.maximum(m, jnp.max(s, axis=0, keepdims=True))
        p_new = jnp.exp(s_new - m)
        l = jnp.sum(p_new, axis=0, keepdims=True)
        acc = jnp.sum(p_new * vn_ref[0], axis=0, keepdims=True)
        for n, s in enumerate(scores):
            p = jnp.exp(s - m)
            l = l + jnp.sum(p, axis=0, keepdims=True)
            acc = acc + jnp.sum(p * vbuf[slot, t * n_sel_pages + n], axis=0, keepdims=True)
        o_ref[0, t:t + 1, :] = acc / l


def _ffn_hosting_attention_kernel(idx_ref, pt_ref, slopes_ref, x_ref, g_ref, w1_ref, w2_ref,
                                  q_ref, kn_ref, vn_ref, ck_hbm, cv_hbm, o_ref, att_ref,
                                  xn_ref, kbuf, vbuf, sems, *, n_sel_pages, ppb, n_pages):
    i = pl.program_id(0)
    j = pl.program_id(1)
    step = i * pl.num_programs(1) + j
    n_steps = pl.num_programs(0) * pl.num_programs(1)
    _sample_pages_fetch(step, n_steps, idx_ref, pt_ref, ck_hbm, cv_hbm, kbuf, vbuf, sems,
                        n_t=q_ref.shape[1], n_sel_pages=n_sel_pages, ppb=ppb, n_pages=n_pages)

    _ffn_step(x_ref, g_ref, w1_ref[...], w2_ref[...], o_ref, xn_ref)
    _sample_attention_step(step, idx_ref, slopes_ref[step % N_HEADS], q_ref, kn_ref, vn_ref,
                           att_ref, kbuf, vbuf, n_sel_pages=n_sel_pages, ppb=ppb, n_pages=n_pages)


def _ffn_hosting_attention(x, g, w1, w2, q, k_new, v_new, cache_k, cache_v, idx_flat, pt_flat,
                           slopes, n_pages, *, tm, tf):
    rows, d = x.shape
    dff = w1.shape[-1]
    db, t, _ = q.shape
    page = cache_k.shape[1]
    ppb = MOBA_BLOCK // page
    n_sel_pages = MOBA_TOP_K * ppb
    n_j = dff // tf
    assert (rows // tm) * n_j == db * N_HEADS

    def new_spec():
        return pl.BlockSpec((1, t, HEAD_DIM),
                            lambda i, j, idx, pt: ((i * n_j + j) // N_HEADS, 0, (i * n_j + j) % N_HEADS))

    kern = functools.partial(_ffn_hosting_attention_kernel, n_sel_pages=n_sel_pages, ppb=ppb,
                             n_pages=n_pages)
    return pl.pallas_call(
        kern,
        grid_spec=pltpu.PrefetchScalarGridSpec(
            num_scalar_prefetch=2,
            grid=(rows // tm, n_j),
            in_specs=[pl.BlockSpec(memory_space=pltpu.SMEM),
                      pl.BlockSpec((tm, d), lambda i, j, idx, pt: (i, 0)),
                      pl.BlockSpec((1, d), lambda i, j, idx, pt: (0, 0)),
                      pl.BlockSpec((d, tf), lambda i, j, idx, pt: (0, j)),
                      pl.BlockSpec((tf, d), lambda i, j, idx, pt: (j, 0)),
                      new_spec(), new_spec(), new_spec(),
                      pl.BlockSpec(memory_space=pl.ANY), pl.BlockSpec(memory_space=pl.ANY)],
            out_specs=[pl.BlockSpec((tm, d), lambda i, j, idx, pt: (i, 0)), new_spec()],
            scratch_shapes=[pltpu.VMEM((tm, d), BF16),
                            pltpu.VMEM((2, t * n_sel_pages, page, HEAD_DIM), F32),
                            pltpu.VMEM((2, t * n_sel_pages, page, HEAD_DIM), F32),
                            pltpu.SemaphoreType.DMA((2, 2))]),
        out_shape=[jax.ShapeDtypeStruct((rows, d), F32), jax.ShapeDtypeStruct(q.shape, F32)],
        compiler_params=_params(("arbitrary", "arbitrary")),
        name="ffn_hosting_attention",
    )(idx_flat, pt_flat, slopes, x, g.reshape(1, d), w1, w2, q, k_new, v_new, cache_k, cache_v)


def kernel(x_prompt, x_sample, cache_k, cache_v, page_table, norm_mix_g, norm_ffn_g,
           gmlp_w_in, gmlp_ln_g, gmlp_ln_b, gmlp_w_s, gmlp_b_s, gmlp_w_out,
           kv_norm_g, w_k, w_v, attn_w_q, attn_w_o, ffn_w1, ffn_w2, final_norm_g):
    bp, sp, d = x_prompt.shape
    bs, ss, _ = x_sample.shape
    depth = norm_mix_g.shape[0]
    n_a = gmlp_w_in.shape[0]
    _, page, n_kv, dh = cache_k.shape
    n_pages = page_table.shape[1]
    assert sp % MOBA_BLOCK == 0 and sp % CHUNK == 0 and ss <= CHUNK
    assert (n_pages * page) % MOBA_BLOCK == 0
    assert n_kv == N_HEADS and dh == HEAD_DIM

    slopes = jnp.exp2(-8.0 * jnp.arange(1, N_HEADS + 1, dtype=F32) / N_HEADS)

    rows_p, rows_s = bp * sp, bs * ss
    xp = x_prompt.reshape(rows_p, d)
    xs = x_sample.reshape(rows_s, d)
    pt_flat = page_table.reshape(-1)

    tm_p = 512
    assert n_a >= 2
    sgu_p, sgu_s = [], []
    att_s = None
    for layer in range(depth):
        g = norm_mix_g[layer]
        last = layer == depth - 1
        if layer < n_a:
            ws_s = jnp.tile(gmlp_w_s[layer][:, :ss, :ss], (1, bs, bs))
            bs_s = jnp.tile(gmlp_b_s[layer][:, :ss].T, (bs, 1))
            xs, vs, w_in, w_out = _gmlp(
                xs, g, gmlp_w_in, gmlp_ln_g[layer], gmlp_ln_b[layer], ws_s, bs_s, gmlp_w_out,
                tm=rows_s, tn=512, chunk_len=ss, v_period=1, layer=layer, emit_bf16=True)
            sgu_s.append(vs.reshape(bs, ss, D_GMLP))
        else:
            a = layer - n_a
            xs, wo_b = _proj_residual(att_s.reshape(rows_s, d), attn_w_o, xs, tm=rows_s, tn=512,
                                      layer=a, emit_bf16=True)
        xs, w1_b, w2_b = _ffn(xs, norm_ffn_g[layer], ffn_w1, ffn_w2, final_norm_g, tm=rows_s,
                              tf=512, final_norm=last, layer=layer, emit_bf16=True)

        ahead = None
        if layer + 1 >= n_a and not last:
            if layer + 1 == n_a:
                ks_, vs_, wk_b, wv_b = _norm_matmul(xs, kv_norm_g, [w_k, w_v], tm=rows_s, tn=512,
                                                   emit_bf16=True)
                kmean_s = kmean_s.transpose(0, 2, 1, 3)
            qs, wq_next = _norm_matmul(xs, norm_mix_g[layer + 1], [attn_w_q], tm=rows_s, tn=512,
                                       layer=layer + 1 - n_a, emit_bf16=True)
            qs3 = qs.reshape(bs, ss, d)
            ahead = (qs3, _sample_topk(qs3, kmean_s).reshape(-1))

        if layer < n_a:
            xp, vp = _gmlp(xp, g, w_in, gmlp_ln_g[layer], gmlp_ln_b[layer], gmlp_w_s[layer],
                           gmlp_b_s[layer].T, w_out, tm=tm_p, tn=1024, chunk_len=CHUNK,
                           v_period=sp // tm_p)
            sgu_p.append(vp.reshape(bp, CHUNK, D_GMLP))
        else:
            if layer == n_a:
                kp, vp_, kb_p, vb_p, kmean_p = _kv_proj(xp, kv_norm_g, wk_b, wv_b, tm=256, tn=d)
            (qp,) = _norm_matmul(xp, g, [wq_b], tm=tm_p, tn=d)
            att_p = _moba_prompt(qp.reshape(bp, sp, d), kb_p.reshape(bp, sp, d),
                                 vb_p.reshape(bp, sp, d),
                                 kmean_p.reshape(bp, sp // MOBA_BLOCK, d), slopes)
            (xp,) = _proj_residual(att_p.reshape(rows_p, d), wo_b, xp, tm=tm_p, tn=d)
        if layer == n_a - 2:
            xp, kmean_s = _ffn_hosting_means(xp, norm_ffn_g[layer], w1_b, w2_b, cache_k, pt_flat,
                                             bs, n_pages, tm=tm_p, tf=512)
        elif ahead is None:
            (xp,) = _ffn(xp, norm_ffn_g[layer], w1_b, w2_b, final_norm_g, tm=tm_p, tf=1024,
                         final_norm=last)
        else:
            xp, att_s = _ffn_hosting_attention(
                xp, norm_ffn_g[layer], w1_b, w2_b, ahead[0], ks_.reshape(bs, ss, d),
                vs_.reshape(bs, ss, d), cache_k, cache_v, ahead[1], pt_flat, slopes, n_pages,
                tm=tm_p, tf=1024)
            wq_b = wq_next

    kv_shape_p = (bp, sp, n_kv, dh)
    kv_shape_s = (bs, ss, n_kv, dh)
    return (xp.reshape(bp, sp, d), xs.reshape(bs, ss, d),
            kp.reshape(kv_shape_p), vp_.reshape(kv_shape_p),
            ks_.reshape(kv_shape_s), vs_.reshape(kv_shape_s),
            jnp.stack(sgu_p), jnp.stack(sgu_s))
```

```python
import functools

import jax
import jax.numpy as jnp
from jax import lax
from jax.experimental import pallas as pl
from jax.experimental.pallas import tpu as pltpu

D_MODEL = 2048
N_HEADS = 16
HEAD_DIM = 128
MOBA_BLOCK = 256
MOBA_TOP_K = 3
CHUNK = 128
D_GMLP = 2 * D_MODEL
N_SGU_GROUPS = 16
SGU_GROUP = D_GMLP // N_SGU_GROUPS
D_FF = 4 * D_MODEL
NORM_EPS = 1e-6

V7X_VMEM_LIMIT_BYTES = 56 * 1024 * 1024

F32 = jnp.float32
BF16 = jnp.bfloat16
NEG_INF = float("-inf")
LOG2E = 1.4426950408889634


def _params(semantics):
    return pltpu.CompilerParams(dimension_semantics=semantics,
                                vmem_limit_bytes=V7X_VMEM_LIMIT_BYTES)


def _rmsnorm_rows(x, g):
    return x * lax.rsqrt(jnp.mean(x * x, axis=-1, keepdims=True) + NORM_EPS) * g


def _dot(a, b):
    return jnp.dot(a, b, preferred_element_type=F32)


def _dot_nt(a, b, precision=None):
    return lax.dot_general(a, b, (((1,), (1,)), ((), ())), precision=precision,
                           preferred_element_type=F32)


def _weight_spec(w, layer, block, index_map):
    mode = {"pipeline_mode": pl.Buffered(1)} if tuple(block) == tuple(w.shape[-2:]) else {}
    if w.ndim == 2:
        return pl.BlockSpec(block, index_map, **mode)
    return pl.BlockSpec((None,) + block, lambda *args: (layer,) + tuple(index_map(*args)), **mode)


def _use_weight(w_ref, wb_ref):
    w = w_ref[...].astype(BF16)
    if wb_ref is not None:
        wb_ref[...] = w
    return w


def _norm_matmul_kernel(*refs, n_w, emit_bf16):
    x_ref, g_ref = refs[:2]
    w_refs = refs[2:2 + n_w]
    o_refs = refs[2 + n_w:2 + 2 * n_w]
    wb_refs = refs[2 + 2 * n_w:2 + 3 * n_w] if emit_bf16 else [None] * n_w
    xn_ref = refs[-1]

    @pl.when(pl.program_id(1) == 0)
    def _():
        xn_ref[...] = _rmsnorm_rows(x_ref[...], g_ref[...]).astype(BF16)

    xn = xn_ref[...]
    for w_ref, o_ref, wb_ref in zip(w_refs, o_refs, wb_refs):
        o_ref[...] = _dot(xn, _use_weight(w_ref, wb_ref))


def _norm_matmul(x, g, ws, *, tm, tn, layer=0, emit_bf16=False):
    rows, d = x.shape
    n = ws[0].shape[-1]
    n_w = len(ws)
    out_specs = [pl.BlockSpec((tm, tn), lambda i, j: (i, j)) for _ in ws]
    out_shape = [jax.ShapeDtypeStruct((rows, n), F32) for _ in ws]
    if emit_bf16:
        assert rows == tm
        out_specs += [pl.BlockSpec((d, tn), lambda i, j: (0, j)) for _ in ws]
        out_shape += [jax.ShapeDtypeStruct((d, n), BF16) for _ in ws]
    return pl.pallas_call(
        functools.partial(_norm_matmul_kernel, n_w=n_w, emit_bf16=emit_bf16),
        grid=(rows // tm, n // tn),
        in_specs=[pl.BlockSpec((tm, d), lambda i, j: (i, 0)),
                  pl.BlockSpec((1, d), lambda i, j: (0, 0))]
                 + [_weight_spec(w, layer, (d, tn), lambda i, j: (0, j)) for w in ws],
        out_specs=out_specs,
        out_shape=out_shape,
        scratch_shapes=[pltpu.VMEM((tm, d), BF16)],
        compiler_params=_params(("parallel", "arbitrary")),
        name="norm_matmul",
    )(x, g.reshape(1, d), *ws)


def _proj_residual_kernel(a_ref, w_ref, x_ref, o_ref, wb_ref=None):
    o_ref[...] = x_ref[...] + _dot(a_ref[...].astype(BF16), _use_weight(w_ref, wb_ref))


def _proj_residual(a, w, x, *, tm, tn, layer=0, emit_bf16=False):
    rows, k = a.shape
    n = w.shape[-1]
    out_specs = [pl.BlockSpec((tm, tn), lambda i, j: (i, j))]
    out_shape = [jax.ShapeDtypeStruct((rows, n), F32)]
    if emit_bf16:
        assert rows == tm
        out_specs.append(pl.BlockSpec((k, tn), lambda i, j: (0, j)))
        out_shape.append(jax.ShapeDtypeStruct((k, n), BF16))
    return pl.pallas_call(
        _proj_residual_kernel,
        grid=(rows // tm, n // tn),
        in_specs=[pl.BlockSpec((tm, k), lambda i, j: (i, 0)),
                  _weight_spec(w, layer, (k, tn), lambda i, j: (0, j)),
                  pl.BlockSpec((tm, tn), lambda i, j: (i, j))],
        out_specs=out_specs,
        out_shape=out_shape,
        compiler_params=_params(("parallel", "parallel")),
        name="proj_residual",
    )(a, w, x)


def _ffn_step(x_ref, g_ref, w1_ref, w2_ref, o_ref, xn_ref, w1b_ref=None, w2b_ref=None):
    @pl.when(pl.program_id(1) == 0)
    def _():
        x = x_ref[...]
        xn_ref[...] = _rmsnorm_rows(x, g_ref[...]).astype(BF16)
        o_ref[...] = x

    h = jnp.maximum(_dot(xn_ref[...], _use_weight(w1_ref, w1b_ref)), 0.0)
    o_ref[...] += _dot((h * h).astype(BF16), _use_weight(w2_ref, w2b_ref))


def _ffn_kernel(x_ref, g_ref, w1_ref, w2_ref, gf_ref, o_ref, *rest, final_norm, emit_bf16):
    w1b_ref, w2b_ref = rest[:2] if emit_bf16 else (None, None)
    xn_ref = rest[-1]
    j = pl.program_id(1)
    _ffn_step(x_ref, g_ref, w1_ref, w2_ref, o_ref, xn_ref, w1b_ref, w2b_ref)

    if final_norm:
        @pl.when(j == pl.num_programs(1) - 1)
        def _():
            o_ref[...] = _rmsnorm_rows(o_ref[...], gf_ref[...])


def _ffn(x, g, w1, w2, gf, *, tm, tf, final_norm, layer=0, emit_bf16=False):
    rows, d = x.shape
    dff = w1.shape[-1]
    out_specs = [pl.BlockSpec((tm, d), lambda i, j: (i, 0))]
    out_shape = [jax.ShapeDtypeStruct((rows, d), F32)]
    if emit_bf16:
        assert rows == tm
        out_specs += [pl.BlockSpec((d, tf), lambda i, j: (0, j)),
                      pl.BlockSpec((tf, d), lambda i, j: (j, 0))]
        out_shape += [jax.ShapeDtypeStruct((d, dff), BF16), jax.ShapeDtypeStruct((dff, d), BF16)]
    return pl.pallas_call(
        functools.partial(_ffn_kernel, final_norm=final_norm, emit_bf16=emit_bf16),
        grid=(rows // tm, dff // tf),
        in_specs=[pl.BlockSpec((tm, d), lambda i, j: (i, 0)),
                  pl.BlockSpec((1, d), lambda i, j: (0, 0)),
                  _weight_spec(w1, layer, (d, tf), lambda i, j: (0, j)),
                  _weight_spec(w2, layer, (tf, d), lambda i, j: (j, 0)),
                  pl.BlockSpec((1, d), lambda i, j: (0, 0))],
        out_specs=out_specs,
        out_shape=out_shape,
        scratch_shapes=[pltpu.VMEM((tm, d), BF16)],
        compiler_params=_params(("parallel", "arbitrary")),
        name="ffn",
    )(x, g.reshape(1, d), w1, w2, gf.reshape(1, d))


def _gmlp_kernel(x_ref, g_ref, win_ref, lng_ref, lnb_ref, ws_ref, bs_ref, wout_ref,
                 y_ref, vout_ref, *rest, nv, tn, chunk_len, mix_rows, emit_bf16):
    winb_ref, woutb_ref = rest[:2] if emit_bf16 else (None, None)
    xn_ref, v_ref, wsb_ref = rest[-3:]
    s = pl.program_id(1)
    tm = x_ref.shape[0]

    @pl.when(s == 0)
    def _():
        x = x_ref[...]
        xn_ref[...] = _rmsnorm_rows(x, g_ref[...]).astype(BF16)
        y_ref[...] = x

    @pl.when(s < nv)
    def _():
        z = jax.nn.gelu(_dot(xn_ref[...], _use_weight(win_ref, winb_ref)))
        v_ref[:, pl.ds(pl.multiple_of(s * tn, tn), tn)] = z

    @pl.when(s == nv)
    def _():
        r = lax.broadcasted_iota(jnp.int32, (mix_rows, mix_rows), 0)
        c = lax.broadcasted_iota(jnp.int32, (mix_rows, mix_rows), 1)
        keep = (r // chunk_len == c // chunk_len) & (c <= r)
        for grp in range(N_SGU_GROUPS):
            wsb_ref[grp] = jnp.where(keep, ws_ref[grp], 0.0).astype(BF16)
        ln_g = lng_ref[...]
        ln_b = lnb_ref[...]

        def mix_tile(t, carry):
            rows = pl.ds(pl.multiple_of(t * mix_rows, mix_rows), mix_rows)
            v = v_ref[rows, :]
            mu = jnp.mean(v, axis=-1, keepdims=True)
            vc = v - mu
            var = jnp.mean(vc * vc, axis=-1, keepdims=True)
            vn = vc * lax.rsqrt(var + NORM_EPS) * ln_g + ln_b
            vout_ref[...] = vn
            for grp in range(N_SGU_GROUPS):
                cols = slice(grp * SGU_GROUP, (grp + 1) * SGU_GROUP)
                v_ref[rows, cols] = (_dot(wsb_ref[grp], vn[:, cols].astype(BF16))
                                     + bs_ref[:, grp:grp + 1])
            return carry

        lax.fori_loop(0, tm // mix_rows, mix_tile, 0)

    @pl.when(s >= nv)
    def _():
        cols = pl.ds(pl.multiple_of((s - nv) * tn, tn), tn)
        u = jax.nn.gelu(_dot(xn_ref[...], _use_weight(win_ref, winb_ref)))
        h = (u * v_ref[:, cols]).astype(BF16)
        y_ref[...] += _dot(h, _use_weight(wout_ref, woutb_ref))


def _gmlp(x, g, w_in, ln_g, ln_b, ws_tiled, bs_tiled, w_out, *, tm, tn, chunk_len,
          v_period, layer=0, emit_bf16=False):
    rows, d = x.shape
    mix_rows = ws_tiled.shape[1]
    nv = D_GMLP // tn
    n_vblocks = rows // (tm * v_period)
    win_tile = lambda i, s: (0, jnp.where(s < nv, s + nv, s - nv))
    wout_tile = lambda i, s: (jnp.maximum(s - nv, 0), 0)
    out_specs = [pl.BlockSpec((tm, d), lambda i, s: (i, 0)),
                 pl.BlockSpec((mix_rows, D_GMLP), lambda i, s: (i // v_period, 0))]
    out_shape = [jax.ShapeDtypeStruct((rows, d), F32),
                 jax.ShapeDtypeStruct((n_vblocks * mix_rows, D_GMLP), F32)]
    if emit_bf16:
        assert rows == tm
        out_specs += [pl.BlockSpec((d, tn), win_tile), pl.BlockSpec((tn, d), wout_tile)]
        out_shape += [jax.ShapeDtypeStruct((d, 2 * D_GMLP), BF16),
                      jax.ShapeDtypeStruct((D_GMLP, d), BF16)]
    kern = functools.partial(_gmlp_kernel, nv=nv, tn=tn, chunk_len=chunk_len,
                             mix_rows=mix_rows, emit_bf16=emit_bf16)
    return pl.pallas_call(
        kern,
        grid=(rows // tm, 2 * nv),
        in_specs=[pl.BlockSpec((tm, d), lambda i, s: (i, 0)),
                  pl.BlockSpec((1, d), lambda i, s: (0, 0)),
                  _weight_spec(w_in, layer, (d, tn), win_tile),
                  pl.BlockSpec((1, D_GMLP), lambda i, s: (0, 0)),
                  pl.BlockSpec((1, D_GMLP), lambda i, s: (0, 0)),
                  pl.BlockSpec((N_SGU_GROUPS, mix_rows, mix_rows), lambda i, s: (0, 0, 0)),
                  pl.BlockSpec((mix_rows, N_SGU_GROUPS), lambda i, s: (0, 0)),
                  _weight_spec(w_out, layer, (tn, d), wout_tile)],
        out_specs=out_specs,
        out_shape=out_shape,
        scratch_shapes=[pltpu.VMEM((tm, d), BF16),
                        pltpu.VMEM((tm, D_GMLP), F32),
                        pltpu.VMEM((N_SGU_GROUPS, mix_rows, mix_rows), BF16)],
        compiler_params=_params(("arbitrary", "arbitrary")),
        name="gmlp",
    )(x, g.reshape(1, d), w_in, ln_g.reshape(1, D_GMLP), ln_b.reshape(1, D_GMLP),
      ws_tiled, bs_tiled, w_out)


def _top_blocks(gate, blk, n_take, axis):
    nb = gate.shape[axis]
    picks = []
    sel = jnp.zeros_like(gate)
    for r in range(MOBA_TOP_K):
        m = jnp.max(gate, axis=axis, keepdims=True)
        idx = jnp.min(jnp.where(gate == m, blk, float(nb)), axis=axis, keepdims=True)
        hit = blk == idx
        picks.append(idx)
        counts = 1.0 if n_take is None else jnp.where(r < n_take, 1.0, 0.0)
        sel = jnp.maximum(sel, jnp.where(hit, counts, 0.0))
        gate = jnp.where(hit, NEG_INF, gate)
    return picks, sel


def _kv_proj_kernel(x_ref, g_ref, wk_ref, wv_ref, k_ref, v_ref, kb_ref, vb_ref, kmean_ref, xn_ref):
    @pl.when(pl.program_id(1) == 0)
    def _():
        xn_ref[...] = _rmsnorm_rows(x_ref[...], g_ref[...]).astype(BF16)

    xn = xn_ref[...]
    k = _dot(xn, wk_ref[...])
    v = _dot(xn, wv_ref[...])
    k_ref[...] = k
    v_ref[...] = v
    kb_ref[...] = k.astype(BF16)
    vb_ref[...] = v.astype(BF16)
    for blk in range(kmean_ref.shape[0]):
        rows = slice(blk * MOBA_BLOCK, (blk + 1) * MOBA_BLOCK)
        kmean_ref[blk] = jnp.mean(k[rows, :], axis=0, keepdims=True)


def _kv_proj(x, g, wk, wv, *, tm, tn):
    rows, d = x.shape
    n = wk.shape[1]
    blocks_per_tile = tm // MOBA_BLOCK
    tile = lambda: pl.BlockSpec((tm, tn), lambda i, j: (i, j))
    return pl.pallas_call(
        _kv_proj_kernel,
        grid=(rows // tm, n // tn),
        in_specs=[pl.BlockSpec((tm, d), lambda i, j: (i, 0)),
                  pl.BlockSpec((1, d), lambda i, j: (0, 0)),
                  _weight_spec(wk, 0, (d, tn), lambda i, j: (0, j)),
                  _weight_spec(wv, 0, (d, tn), lambda i, j: (0, j))],
        out_specs=[tile(), tile(), tile(), tile(),
                   pl.BlockSpec((blocks_per_tile, 1, tn), lambda i, j: (i, 0, j))],
        out_shape=[jax.ShapeDtypeStruct((rows, n), F32), jax.ShapeDtypeStruct((rows, n), F32),
                   jax.ShapeDtypeStruct((rows, n), BF16), jax.ShapeDtypeStruct((rows, n), BF16),
                   jax.ShapeDtypeStruct((rows // MOBA_BLOCK, 1, n), F32)],
        scratch_shapes=[pltpu.VMEM((tm, d), BF16)],
        compiler_params=_params(("parallel", "arbitrary")),
        name="kv_proj",
    )(x, g.reshape(1, d), wk, wv)


HEADS_PER_STEP = 8
BLOCKS_PER_TRIP = 4


def _moba_prompt_kernel(slopes_ref, q_ref, kb_ref, vb_ref, kmean_ref, o_ref,
                        vt_ref, qaug_ref, kaug_ref, sel_ref, acc_ref):
    hg = pl.program_id(1)
    qt = pl.program_id(2)
    n_blocks = kb_ref.shape[1] // MOBA_BLOCK
    score_scale = (HEAD_DIM ** -0.5) * LOG2E
    heads = range(HEADS_PER_STEP)
    cols = [slice(hh * HEAD_DIM, (hh + 1) * HEAD_DIM) for hh in heads]
    slope2 = [slopes_ref[hg * HEADS_PER_STEP + hh] * LOG2E for hh in heads]

    @pl.when(qt == 0)
    def _():
        for j in range(n_blocks):
            rows = slice(j * MOBA_BLOCK, (j + 1) * MOBA_BLOCK)
            vt_ref[:, rows] = vb_ref[0, rows, :].astype(F32).T.astype(BF16)
        key_i = lax.broadcasted_iota(jnp.int32, (MOBA_BLOCK, HEAD_DIM), 0)
        piece = lax.broadcasted_iota(jnp.int32, (MOBA_BLOCK, HEAD_DIM), 1)
        kaug_ref[...] = jnp.where(piece < 3, key_i, 0).astype(BF16)
        part = lax.broadcasted_iota(jnp.int32, (HEAD_DIM, MOBA_BLOCK), 0)
        for hh in heads:
            whole = jnp.full((HEAD_DIM, MOBA_BLOCK), slope2[hh], F32)
            hi = whole.astype(BF16).astype(F32)
            mid = (whole - hi).astype(BF16).astype(F32)
            lo = whole - hi - mid
            pieces = jnp.where(part == 0, hi, jnp.where(part == 1, mid, jnp.where(part == 2, lo, 0.0)))
            qaug_ref[hh, HEAD_DIM:, :] = pieces.astype(BF16)

    key_i = lax.broadcasted_iota(jnp.int32, (MOBA_BLOCK, MOBA_BLOCK), 0)
    qry_i = lax.broadcasted_iota(jnp.int32, (MOBA_BLOCK, MOBA_BLOCK), 1)
    causal = qry_i >= key_i
    own = pl.ds(pl.multiple_of(qt * MOBA_BLOCK, MOBA_BLOCK), MOBA_BLOCK)
    q_ts = [q_ref[0, :, cols[hh]].T for hh in heads]
    for hh in heads:
        qaug_ref[hh, :HEAD_DIM, :] = (q_ts[hh] * score_scale).astype(BF16)
    kaug = kaug_ref[...]

    def scores(rows, hh):
        keys = jnp.concatenate([kb_ref[0, rows, cols[hh]], kaug], axis=1)
        return _dot(keys, qaug_ref[hh])

    raw = [scores(own, hh) for hh in heads]
    gates = [jnp.dot(kmean_ref[0, :, cols[hh]], q_ts[hh], precision=lax.Precision.HIGHEST,
                     preferred_element_type=F32) for hh in heads]
    stats = []
    for hh in heads:
        blk = lax.broadcasted_iota(jnp.int32, gates[hh].shape, 0)
        gate = jnp.where(blk < qt, gates[hh], NEG_INF)
        _, sel = _top_blocks(gate, blk.astype(F32), qt, axis=0)
        sel_ref[hh] = sel
        t = jnp.where(causal, raw[hh], NEG_INF)
        m0 = jnp.max(t, axis=0, keepdims=True)
        p = jnp.exp2(t - m0)
        l0 = jnp.sum(p, axis=0, keepdims=True)
        acc_ref[hh] = _dot(vt_ref[cols[hh], own], p.astype(BF16))
        stats.append((m0, l0))

    def past_blocks(j0, n_now, stats):
        blocks = range(n_now)
        rows = [pl.ds(pl.multiple_of((j0 + i) * MOBA_BLOCK, MOBA_BLOCK), MOBA_BLOCK) for i in blocks]
        raw = [[scores(rows[i], hh) for hh in heads] for i in blocks]
        stats = list(stats)
        for i in blocks:
            blocks_between = ((qt - j0 - i) * MOBA_BLOCK).astype(F32)
            for hh in heads:
                m, l = stats[hh]
                t = raw[i][hh]
                far = slope2[hh] * blocks_between
                picked = sel_ref[hh, pl.ds(j0 + i, 1), :] > 0.0
                m_blk = jnp.max(t, axis=0, keepdims=True) - far
                m_new = jnp.where(picked, jnp.maximum(m, m_blk), m)
                shift = jnp.where(picked, m_new + far, jnp.inf)
                p = jnp.exp2(t - shift)
                alpha = jnp.exp2(m - m_new)
                l = alpha * l + jnp.sum(p, axis=0, keepdims=True)
                acc_ref[hh] = alpha * acc_ref[hh] + _dot(vt_ref[cols[hh], rows[i]], p.astype(BF16))
                stats[hh] = (m_new, l)
        return tuple(stats)

    stats = lax.fori_loop(0, qt // BLOCKS_PER_TRIP,
                          lambda i, st: past_blocks(i * BLOCKS_PER_TRIP, BLOCKS_PER_TRIP, st),
                          tuple(stats))
    done = (qt // BLOCKS_PER_TRIP) * BLOCKS_PER_TRIP
    stats = lax.fori_loop(done, qt, lambda j, st: past_blocks(j, 1, st), stats)
    for hh in heads:
        o_ref[0, :, cols[hh]] = (acc_ref[hh] / stats[hh][1]).T.astype(o_ref.dtype)


def _moba_prompt(q, kb, vb, kmean, slopes):
    bsz, seq, _ = q.shape
    n_qt = seq // MOBA_BLOCK
    width = HEADS_PER_STEP * HEAD_DIM
    return pl.pallas_call(
        _moba_prompt_kernel,
        grid_spec=pltpu.PrefetchScalarGridSpec(
            num_scalar_prefetch=0,
            grid=(bsz, N_HEADS // HEADS_PER_STEP, n_qt),
            in_specs=[pl.BlockSpec(memory_space=pltpu.SMEM),
                      pl.BlockSpec((1, MOBA_BLOCK, width), lambda b, h, t: (b, t, h)),
                      pl.BlockSpec((1, seq, width), lambda b, h, t: (b, 0, h),
                                   pipeline_mode=pl.Buffered(1)),
                      pl.BlockSpec((1, seq, width), lambda b, h, t: (b, 0, h),
                                   pipeline_mode=pl.Buffered(1)),
                      pl.BlockSpec((1, n_qt, width), lambda b, h, t: (b, 0, h))],
            out_specs=pl.BlockSpec((1, MOBA_BLOCK, width), lambda b, h, t: (b, t, h)),
            scratch_shapes=[pltpu.VMEM((width, seq), BF16),
                            pltpu.VMEM((HEADS_PER_STEP, 2 * HEAD_DIM, MOBA_BLOCK), BF16),
                            pltpu.VMEM((MOBA_BLOCK, HEAD_DIM), BF16),
                            pltpu.VMEM((HEADS_PER_STEP, n_qt, MOBA_BLOCK), F32),
                            pltpu.VMEM((HEADS_PER_STEP, HEAD_DIM, MOBA_BLOCK), F32)]),
        out_shape=jax.ShapeDtypeStruct(q.shape, BF16),
        compiler_params=_params(("parallel", "parallel", "arbitrary")),
        name="moba_prompt",
    )(slopes, q, kb, vb, kmean)


def _page_block_means(page_refs, o_ref, ppb):
    for blk in range(len(page_refs) // ppb):
        total = jnp.sum(page_refs[blk * ppb][0], axis=0)
        for p in range(1, ppb):
            total = total + jnp.sum(page_refs[blk * ppb + p][0], axis=0)
        o_ref[0, blk] = total / MOBA_BLOCK


def _ffn_hosting_means_kernel(pt_ref, x_ref, g_ref, w1_ref, w2_ref, *rest, n_page_refs, ppb):
    del pt_ref
    page_refs = rest[:n_page_refs]
    o_ref, kmean_ref, xn_ref = rest[n_page_refs:]
    _ffn_step(x_ref, g_ref, w1_ref, w2_ref, o_ref, xn_ref)
    _page_block_means(page_refs, kmean_ref, ppb)


def _ffn_hosting_means(x, g, w1, w2, cache_k, pt_flat, db, n_pages, *, tm, tf):
    rows, d = x.shape
    dff = w1.shape[-1]
    _, page, n_kv, dh = cache_k.shape
    ppb = MOBA_BLOCK // page
    nfb = (n_pages * page) // MOBA_BLOCK
    n_j = dff // tf
    n_steps = (rows // tm) * n_j
    assert (db * nfb) % n_steps == 0 and nfb % ((db * nfb) // n_steps) == 0
    blocks_per_step = (db * nfb) // n_steps
    pages_per_step = blocks_per_step * ppb
    steps_per_batch = nfb // blocks_per_step

    def page_spec(p):
        def index_map(i, j, pt):
            step = i * n_j + j
            logical = (step % steps_per_batch) * pages_per_step + p
            return (pt[(step // steps_per_batch) * n_pages + logical], 0, 0, 0)
        return pl.BlockSpec((1, page, n_kv, dh), index_map)

    def means_map(i, j, pt):
        step = i * n_j + j
        return (step // steps_per_batch, step % steps_per_batch, 0, 0)

    kern = functools.partial(_ffn_hosting_means_kernel, n_page_refs=pages_per_step, ppb=ppb)
    return pl.pallas_call(
        kern,
        grid_spec=pltpu.PrefetchScalarGridSpec(
            num_scalar_prefetch=1,
            grid=(rows // tm, n_j),
            in_specs=[pl.BlockSpec((tm, d), lambda i, j, pt: (i, 0)),
                      pl.BlockSpec((1, d), lambda i, j, pt: (0, 0)),
                      pl.BlockSpec((d, tf), lambda i, j, pt: (0, j)),
                      pl.BlockSpec((tf, d), lambda i, j, pt: (j, 0))]
                     + [page_spec(p) for p in range(pages_per_step)],
            out_specs=[pl.BlockSpec((tm, d), lambda i, j, pt: (i, 0)),
                       pl.BlockSpec((1, blocks_per_step, n_kv, dh), means_map)],
            scratch_shapes=[pltpu.VMEM((tm, d), BF16)]),
        out_shape=[jax.ShapeDtypeStruct((rows, d), F32),
                   jax.ShapeDtypeStruct((db, nfb, n_kv, dh), F32)],
        compiler_params=_params(("arbitrary", "arbitrary")),
        name="ffn_hosting_means",
    )(pt_flat, x, g.reshape(1, d), w1, w2, *([cache_k] * pages_per_step))


def _sample_topk_kernel(q_ref, km_ref, o_ref):
    t = q_ref.shape[1]
    lane = lax.broadcasted_iota(jnp.int32, (t, 128), 1)
    for h in range(N_HEADS):
        cols = slice(h * HEAD_DIM, (h + 1) * HEAD_DIM)
        gate = _dot_nt(q_ref[0, :, cols], km_ref[0, h], precision=lax.Precision.HIGHEST)
        blk_f = lax.broadcasted_iota(jnp.int32, gate.shape, 1).astype(F32)
        picks, _ = _top_blocks(gate, blk_f, None, axis=1)
        out = jnp.zeros((t, 128), F32)
        for r, idx in enumerate(picks):
            out = jnp.where(lane == r, idx, out)
        o_ref[0, h] = out.astype(jnp.int32)


def _sample_topk(q, kmean):
    db, t, _ = q.shape
    nfb = kmean.shape[2]
    assert nfb >= MOBA_TOP_K
    out = pl.pallas_call(
        _sample_topk_kernel,
        grid=(db,),
        in_specs=[pl.BlockSpec((1, t, D_MODEL), lambda b: (b, 0, 0)),
                  pl.BlockSpec((1, N_HEADS, nfb, HEAD_DIM), lambda b: (b, 0, 0, 0))],
        out_specs=pl.BlockSpec((1, N_HEADS, t, 128), lambda b: (b, 0, 0, 0)),
        out_shape=jax.ShapeDtypeStruct((db, N_HEADS, t, 128), jnp.int32),
        compiler_params=_params(("parallel",)),
        name="sample_topk",
    )(q, kmean)
    return out[..., :MOBA_TOP_K]


def _sample_pages_fetch(step, n_steps, idx_ref, pt_ref, ck_hbm, cv_hbm, kbuf, vbuf, sems,
                        *, n_t, n_sel_pages, ppb, n_pages):
    slot = step % 2

    def page_copies(at_step, sl):
        bb = at_step // N_HEADS
        hh = at_step % N_HEADS
        copies = []
        for t in range(n_t):
            for n in range(n_sel_pages):
                block = idx_ref[(at_step * n_t + t) * MOBA_TOP_K + n // ppb]
                phys = pt_ref[bb * n_pages + block * ppb + n % ppb]
                dst = t * n_sel_pages + n
                copies.append(pltpu.make_async_copy(ck_hbm.at[phys, :, hh, :], kbuf.at[sl, dst],
                                                    sems.at[0, sl]))
                copies.append(pltpu.make_async_copy(cv_hbm.at[phys, :, hh, :], vbuf.at[sl, dst],
                                                    sems.at[1, sl]))
        return copies

    @pl.when(step == 0)
    def _():
        for cp in page_copies(step, slot):
            cp.start()

    @pl.when(step + 1 < n_steps)
    def _():
        for cp in page_copies(step + 1, 1 - slot):
            cp.start()

    for cp in page_copies(step, slot):
        cp.wait()


def _sample_attention_step(step, idx_ref, slope, q_ref, kn_ref, vn_ref, o_ref, kbuf, vbuf,
                           *, n_sel_pages, ppb, n_pages):
    n_t = q_ref.shape[1]
    page = kbuf.shape[2]
    past_len = n_pages * page
    slot = step % 2
    scale = HEAD_DIM ** -0.5
    off_bias = slope * lax.broadcasted_iota(jnp.int32, (page, 1), 0).astype(F32)
    new_i = lax.broadcasted_iota(jnp.int32, (n_t, 1), 0)
    for t in range(n_t):
        q = q_ref[0, t:t + 1, :]
        pos_q = past_len + t
        scores = []
        for n in range(n_sel_pages):
            s = jnp.sum(kbuf[slot, t * n_sel_pages + n] * q, axis=1, keepdims=True) * scale
            block = idx_ref[(step * n_t + t) * MOBA_TOP_K + n // ppb]
            to_page = (pos_q - (block * ppb + n % ppb) * page).astype(F32)
            scores.append(s + off_bias - slope * to_page)
        s_new = jnp.sum(kn_ref[0] * q, axis=1, keepdims=True) * scale
        d_new = t - new_i
        s_new = jnp.where(d_new >= 0, s_new - slope * d_new.astype(F32), NEG_INF)

        m = jnp.max(s_new, axis=0, keepdims=True)
        for s in scores:
            m = jnp.maximum(m, jnp.max(s, axis=0, keepdims=True))
        p_new = jnp.exp(s_new - m)
        l = jnp.sum(p_new, axis=0, keepdims=True)
        acc = jnp.sum(p_new * vn_ref[0], axis=0, keepdims=True)
        for n, s in enumerate(scores):
            p = jnp.exp(s - m)
            l = l + jnp.sum(p, axis=0, keepdims=True)
            acc = acc + jnp.sum(p * vbuf[slot, t * n_sel_pages + n], axis=0, keepdims=True)
        o_ref[0, t:t + 1, :] = acc / l


def _ffn_hosting_attention_kernel(idx_ref, pt_ref, slopes_ref, x_ref, g_ref, w1_ref, w2_ref,
                                  q_ref, kn_ref, vn_ref, ck_hbm, cv_hbm, o_ref, att_ref,
                                  xn_ref, kbuf, vbuf, sems, *, n_sel_pages, ppb, n_pages):
    i = pl.program_id(0)
    j = pl.program_id(1)
    step = i * pl.num_programs(1) + j
    n_steps = pl.num_programs(0) * pl.num_programs(1)
    _sample_pages_fetch(step, n_steps, idx_ref, pt_ref, ck_hbm, cv_hbm, kbuf, vbuf, sems,
                        n_t=q_ref.shape[1], n_sel_pages=n_sel_pages, ppb=ppb, n_pages=n_pages)

    _ffn_step(x_ref, g_ref, w1_ref, w2_ref, o_ref, xn_ref)
    _sample_attention_step(step, idx_ref, slopes_ref[step % N_HEADS], q_ref, kn_ref, vn_ref,
                           att_ref, kbuf, vbuf, n_sel_pages=n_sel_pages, ppb=ppb, n_pages=n_pages)


def _ffn_hosting_attention(x, g, w1, w2, q, k_new, v_new, cache_k, cache_v, idx_flat, pt_flat,
                           slopes, n_pages, *, tm, tf):
    rows, d = x.shape
    dff = w1.shape[-1]
    db, t, _ = q.shape
    page = cache_k.shape[1]
    ppb = MOBA_BLOCK // page
    n_sel_pages = MOBA_TOP_K * ppb
    n_j = dff // tf
    assert (rows // tm) * n_j == db * N_HEADS

    def new_spec():
        return pl.BlockSpec((1, t, HEAD_DIM),
                            lambda i, j, idx, pt: ((i * n_j + j) // N_HEADS, 0, (i * n_j + j) % N_HEADS))

    kern = functools.partial(_ffn_hosting_attention_kernel, n_sel_pages=n_sel_pages, ppb=ppb,
                             n_pages=n_pages)
    return pl.pallas_call(
        kern,
        grid_spec=pltpu.PrefetchScalarGridSpec(
            num_scalar_prefetch=2,
            grid=(rows // tm, n_j),
            in_specs=[pl.BlockSpec(memory_space=pltpu.SMEM),
                      pl.BlockSpec((tm, d), lambda i, j, idx, pt: (i, 0)),
                      pl.BlockSpec((1, d), lambda i, j, idx, pt: (0, 0)),
                      pl.BlockSpec((d, tf), lambda i, j, idx, pt: (0, j)),
                      pl.BlockSpec((tf, d), lambda i, j, idx, pt: (j, 0)),
                      new_spec(), new_spec(), new_spec(),
                      pl.BlockSpec(memory_space=pl.ANY), pl.BlockSpec(memory_space=pl.ANY)],
            out_specs=[pl.BlockSpec((tm, d), lambda i, j, idx, pt: (i, 0)), new_spec()],
            scratch_shapes=[pltpu.VMEM((tm, d), BF16),
                            pltpu.VMEM((2, t * n_sel_pages, page, HEAD_DIM), F32),
                            pltpu.VMEM((2, t * n_sel_pages, page, HEAD_DIM), F32),
                            pltpu.SemaphoreType.DMA((2, 2))]),
        out_shape=[jax.ShapeDtypeStruct((rows, d), F32), jax.ShapeDtypeStruct(q.shape, F32)],
        compiler_params=_params(("arbitrary", "arbitrary")),
        name="ffn_hosting_attention",
    )(idx_flat, pt_flat, slopes, x, g.reshape(1, d), w1, w2, q, k_new, v_new, cache_k, cache_v)


def kernel(x_prompt, x_sample, cache_k, cache_v, page_table, norm_mix_g, norm_ffn_g,
           gmlp_w_in, gmlp_ln_g, gmlp_ln_b, gmlp_w_s, gmlp_b_s, gmlp_w_out,
           kv_norm_g, w_k, w_v, attn_w_q, attn_w_o, ffn_w1, ffn_w2, final_norm_g):
    bp, sp, d = x_prompt.shape
    bs, ss, _ = x_sample.shape
    depth = norm_mix_g.shape[0]
    n_a = gmlp_w_in.shape[0]
    _, page, n_kv, dh = cache_k.shape
    n_pages = page_table.shape[1]
    assert sp % MOBA_BLOCK == 0 and sp % CHUNK == 0 and ss <= CHUNK
    assert (n_pages * page) % MOBA_BLOCK == 0
    assert n_kv == N_HEADS and dh == HEAD_DIM

    slopes = jnp.exp2(-8.0 * jnp.arange(1, N_HEADS + 1, dtype=F32) / N_HEADS)

    rows_p, rows_s = bp * sp, bs * ss
    xp = x_prompt.reshape(rows_p, d)
    xs = x_sample.reshape(rows_s, d)
    pt_flat = page_table.reshape(-1)

    tm_p = 512
    assert n_a >= 2
    sgu_p, sgu_s = [], []
    att_s = None
    for layer in range(depth):
        g = norm_mix_g[layer]
        last = layer == depth - 1
        if layer < n_a:
            ws_s = jnp.tile(gmlp_w_s[layer][:, :ss, :ss], (1, bs, bs))
            bs_s = jnp.tile(gmlp_b_s[layer][:, :ss].T, (bs, 1))
            xs, vs, w_in, w_out = _gmlp(
                xs, g, gmlp_w_in, gmlp_ln_g[layer], gmlp_ln_b[layer], ws_s, bs_s, gmlp_w_out,
                tm=rows_s, tn=512, chunk_len=ss, v_period=1, layer=layer, emit_bf16=True)
            sgu_s.append(vs.reshape(bs, ss, D_GMLP))
        else:
            a = layer - n_a
            xs, wo_b = _proj_residual(att_s.reshape(rows_s, d), attn_w_o, xs, tm=rows_s, tn=512,
                                      layer=a, emit_bf16=True)
        xs, w1_b, w2_b = _ffn(xs, norm_ffn_g[layer], ffn_w1, ffn_w2, final_norm_g, tm=rows_s,
                              tf=512, final_norm=last, layer=layer, emit_bf16=True)

        ahead = None
        if layer + 1 >= n_a and not last:
            if layer + 1 == n_a:
                ks_, vs_, wk_b, wv_b = _norm_matmul(xs, kv_norm_g, [w_k, w_v], tm=rows_s, tn=512,
                                                   emit_bf16=True)
                kmean_s = kmean_s.transpose(0, 2, 1, 3)
            qs, wq_next = _norm_matmul(xs, norm_mix_g[layer + 1], [attn_w_q], tm=rows_s, tn=512,
                                       layer=layer + 1 - n_a, emit_bf16=True)
            qs3 = qs.reshape(bs, ss, d)
            ahead = (qs3, _sample_topk(qs3, kmean_s).reshape(-1))

        if layer < n_a:
            xp, vp = _gmlp(xp, g, w_in, gmlp_ln_g[layer], gmlp_ln_b[layer], gmlp_w_s[layer],
                           gmlp_b_s[layer].T, w_out, tm=tm_p, tn=1024, chunk_len=CHUNK,
                           v_period=sp // tm_p)
            sgu_p.append(vp.reshape(bp, CHUNK, D_GMLP))
        else:
            if layer == n_a:
                kp, vp_, kb_p, vb_p, kmean_p = _kv_proj(xp, kv_norm_g, wk_b, wv_b, tm=256, tn=d)
            (qp,) = _norm_matmul(xp, g, [wq_b], tm=tm_p, tn=d)
            att_p = _moba_prompt(qp.reshape(bp, sp, d), kb_p.reshape(bp, sp, d),
                                 vb_p.reshape(bp, sp, d),
                                 kmean_p.reshape(bp, sp // MOBA_BLOCK, d), slopes)
            (xp,) = _proj_residual(att_p.reshape(rows_p, d), wo_b, xp, tm=tm_p, tn=d)
        if layer == n_a - 2:
            xp, kmean_s = _ffn_hosting_means(xp, norm_ffn_g[layer], w1_b, w2_b, cache_k, pt_flat,
                                             bs, n_pages, tm=tm_p, tf=512)
        elif ahead is None:
            (xp,) = _ffn(xp, norm_ffn_g[layer], w1_b, w2_b, final_norm_g, tm=tm_p, tf=1024,
                         final_norm=last)
        else:
            xp, att_s = _ffn_hosting_attention(
                xp, norm_ffn_g[layer], w1_b, w2_b, ahead[0], ks_.reshape(bs, ss, d),
                vs_.reshape(bs, ss, d), cache_k, cache_v, ahead[1], pt_flat, slopes, n_pages,
                tm=tm_p, tf=1024)
            wq_b = wq_next

    kv_shape_p = (bp, sp, n_kv, dh)
    kv_shape_s = (bs, ss, n_kv, dh)
    return (xp.reshape(bp, sp, d), xs.reshape(bs, ss, d),
            kp.reshape(kv_shape_p), vp_.reshape(kv_shape_p),
            ks_.reshape(kv_shape_s), vs_.reshape(kv_shape_s),
            jnp.stack(sgu_p), jnp.stack(sgu_s))
```

```python
import functools

import jax
import jax.numpy as jnp
from jax import lax
from jax.experimental import pallas as pl
from jax.experimental.pallas import tpu as pltpu

D_MODEL = 2048
N_HEADS = 16
HEAD_DIM = 128
MOBA_BLOCK = 256
MOBA_TOP_K = 3
CHUNK = 128
D_GMLP = 2 * D_MODEL
N_SGU_GROUPS = 16
SGU_GROUP = D_GMLP // N_SGU_GROUPS
D_FF = 4 * D_MODEL
NORM_EPS = 1e-6

V7X_VMEM_LIMIT_BYTES = 56 * 1024 * 1024

F32 = jnp.float32
BF16 = jnp.bfloat16
NEG_INF = float("-inf")
LOG2E = 1.4426950408889634


def _params(semantics):
    return pltpu.CompilerParams(dimension_semantics=semantics,
                                vmem_limit_bytes=V7X_VMEM_LIMIT_BYTES)


def _rmsnorm_rows(x, g):
    return x * lax.rsqrt(jnp.mean(x * x, axis=-1, keepdims=True) + NORM_EPS) * g


def _dot(a, b):
    return jnp.dot(a, b, preferred_element_type=F32)


def _dot_nt(a, b, precision=None):
    return lax.dot_general(a, b, (((1,), (1,)), ((), ())), precision=precision,
                           preferred_element_type=F32)


def _weight_spec(w, layer, block, index_map):
    mode = {"pipeline_mode": pl.Buffered(1)} if tuple(block) == tuple(w.shape[-2:]) else {}
    if w.ndim == 2:
        return pl.BlockSpec(block, index_map, **mode)
    return pl.BlockSpec((None,) + block, lambda *args: (layer,) + tuple(index_map(*args)), **mode)


def _use_weight(w_ref, wb_ref):
    w = w_ref[...].astype(BF16)
    if wb_ref is not None:
        wb_ref[...] = w
    return w


def _norm_matmul_kernel(*refs, n_w, emit_bf16):
    x_ref, g_ref = refs[:2]
    w_refs = refs[2:2 + n_w]
    o_refs = refs[2 + n_w:2 + 2 * n_w]
    wb_refs = refs[2 + 2 * n_w:2 + 3 * n_w] if emit_bf16 else [None] * n_w
    xn_ref = refs[-1]

    @pl.when(pl.program_id(1) == 0)
    def _():
        xn_ref[...] = _rmsnorm_rows(x_ref[...], g_ref[...]).astype(BF16)

    xn = xn_ref[...]
    for w_ref, o_ref, wb_ref in zip(w_refs, o_refs, wb_refs):
        o_ref[...] = _dot(xn, _use_weight(w_ref, wb_ref))


def _norm_matmul(x, g, ws, *, tm, tn, layer=0, emit_bf16=False):
    rows, d = x.shape
    n = ws[0].shape[-1]
    n_w = len(ws)
    out_specs = [pl.BlockSpec((tm, tn), lambda i, j: (i, j)) for _ in ws]
    out_shape = [jax.ShapeDtypeStruct((rows, n), F32) for _ in ws]
    if emit_bf16:
        assert rows == tm
        out_specs += [pl.BlockSpec((d, tn), lambda i, j: (0, j)) for _ in ws]
        out_shape += [jax.ShapeDtypeStruct((d, n), BF16) for _ in ws]
    return pl.pallas_call(
        functools.partial(_norm_matmul_kernel, n_w=n_w, emit_bf16=emit_bf16),
        grid=(rows // tm, n // tn),
        in_specs=[pl.BlockSpec((tm, d), lambda i, j: (i, 0)),
                  pl.BlockSpec((1, d), lambda i, j: (0, 0))]
                 + [_weight_spec(w, layer, (d, tn), lambda i, j: (0, j)) for w in ws],
        out_specs=out_specs,
        out_shape=out_shape,
        scratch_shapes=[pltpu.VMEM((tm, d), BF16)],
        compiler_params=_params(("parallel", "arbitrary")),
        name="norm_matmul",
    )(x, g.reshape(1, d), *ws)


def _proj_residual_kernel(a_ref, w_ref, x_ref, o_ref, wb_ref=None):
    o_ref[...] = x_ref[...] + _dot(a_ref[...].astype(BF16), _use_weight(w_ref, wb_ref))


def _proj_residual(a, w, x, *, tm, tn, layer=0, emit_bf16=False):
    rows, k = a.shape
    n = w.shape[-1]
    out_specs = [pl.BlockSpec((tm, tn), lambda i, j: (i, j))]
    out_shape = [jax.ShapeDtypeStruct((rows, n), F32)]
    if emit_bf16:
        assert rows == tm
        out_specs.append(pl.BlockSpec((k, tn), lambda i, j: (0, j)))
        out_shape.append(jax.ShapeDtypeStruct((k, n), BF16))
    return pl.pallas_call(
        _proj_residual_kernel,
        grid=(rows // tm, n // tn),
        in_specs=[pl.BlockSpec((tm, k), lambda i, j: (i, 0)),
                  _weight_spec(w, layer, (k, tn), lambda i, j: (0, j)),
                  pl.BlockSpec((tm, tn), lambda i, j: (i, j))],
        out_specs=out_specs,
        out_shape=out_shape,
        compiler_params=_params(("parallel", "parallel")),
        name="proj_residual",
    )(a, w, x)


def _ffn_step(x_ref, g_ref, w1_ref, w2_ref, o_ref, xn_ref, w1b_ref=None, w2b_ref=None):
    @pl.when(pl.program_id(1) == 0)
    def _():
        x = x_ref[...]
        xn_ref[...] = _rmsnorm_rows(x, g_ref[...]).astype(BF16)
        o_ref[...] = x

    h = jnp.maximum(_dot(xn_ref[...], _use_weight(w1_ref, w1b_ref)), 0.0)
    o_ref[...] += _dot((h * h).astype(BF16), _use_weight(w2_ref, w2b_ref))


def _ffn_kernel(x_ref, g_ref, w1_ref, w2_ref, gf_ref, o_ref, *rest, final_norm, emit_bf16):
    w1b_ref, w2b_ref = rest[:2] if emit_bf16 else (None, None)
    xn_ref = rest[-1]
    j = pl.program_id(1)
    _ffn_step(x_ref, g_ref, w1_ref, w2_ref, o_ref, xn_ref, w1b_ref, w2b_ref)

    if final_norm:
        @pl.when(j == pl.num_programs(1) - 1)
        def _():
            o_ref[...] = _rmsnorm_rows(o_ref[...], gf_ref[...])


def _ffn(x, g, w1, w2, gf, *, tm, tf, final_norm, layer=0, emit_bf16=False):
    rows, d = x.shape
    dff = w1.shape[-1]
    out_specs = [pl.BlockSpec((tm, d), lambda i, j: (i, 0))]
    out_shape = [jax.ShapeDtypeStruct((rows, d), F32)]
    if emit_bf16:
        assert rows == tm
        out_specs += [pl.BlockSpec((d, tf), lambda i, j: (0, j)),
                      pl.BlockSpec((tf, d), lambda i, j: (j, 0))]
        out_shape += [jax.ShapeDtypeStruct((d, dff), BF16), jax.ShapeDtypeStruct((dff, d), BF16)]
    return pl.pallas_call(
        functools.partial(_ffn_kernel, final_norm=final_norm, emit_bf16=emit_bf16),
        grid=(rows // tm, dff // tf),
        in_specs=[pl.BlockSpec((tm, d), lambda i, j: (i, 0)),
                  pl.BlockSpec((1, d), lambda i, j: (0, 0)),
                  _weight_spec(w1, layer, (d, tf), lambda i, j: (0, j)),
                  _weight_spec(w2, layer, (tf, d), lambda i, j: (j, 0)),
                  pl.BlockSpec((1, d), lambda i, j: (0, 0))],
        out_specs=out_specs,
        out_shape=out_shape,
        scratch_shapes=[pltpu.VMEM((tm, d), BF16)],
        compiler_params=_params(("parallel", "arbitrary")),
        name="ffn",
    )(x, g.reshape(1, d), w1, w2, gf.reshape(1, d))


def _gmlp_kernel(x_ref, g_ref, win_ref, lng_ref, lnb_ref, ws_ref, bs_ref, wout_ref,
                 y_ref, vout_ref, *rest, nv, tn, chunk_len, mix_rows, emit_bf16):
    winb_ref, woutb_ref = rest[:2] if emit_bf16 else (None, None)
    xn_ref, v_ref, wsb_ref = rest[-3:]
    s = pl.program_id(1)
    tm = x_ref.shape[0]

    @pl.when(s == 0)
    def _():
        x = x_ref[...]
        xn_ref[...] = _rmsnorm_rows(x, g_ref[...]).astype(BF16)
        y_ref[...] = x

    @pl.when(s < nv)
    def _():
        z = jax.nn.gelu(_dot(xn_ref[...], _use_weight(win_ref, winb_ref)))
        v_ref[:, pl.ds(pl.multiple_of(s * tn, tn), tn)] = z

    @pl.when(s == nv)
    def _():
        r = lax.broadcasted_iota(jnp.int32, (mix_rows, mix_rows), 0)
        c = lax.broadcasted_iota(jnp.int32, (mix_rows, mix_rows), 1)
        keep = (r // chunk_len == c // chunk_len) & (c <= r)
        for grp in range(N_SGU_GROUPS):
            wsb_ref[grp] = jnp.where(keep, ws_ref[grp], 0.0).astype(BF16)
        ln_g = lng_ref[...]
        ln_b = lnb_ref[...]

        def mix_tile(t, carry):
            rows = pl.ds(pl.multiple_of(t * mix_rows, mix_rows), mix_rows)
            v = v_ref[rows, :]
            mu = jnp.mean(v, axis=-1, keepdims=True)
            vc = v - mu
            var = jnp.mean(vc * vc, axis=-1, keepdims=True)
            vn = vc * lax.rsqrt(var + NORM_EPS) * ln_g + ln_b
            vout_ref[...] = vn
            for grp in range(N_SGU_GROUPS):
                cols = slice(grp * SGU_GROUP, (grp + 1) * SGU_GROUP)
                v_ref[rows, cols] = (_dot(wsb_ref[grp], vn[:, cols].astype(BF16))
                                     + bs_ref[:, grp:grp + 1])
            return carry

        lax.fori_loop(0, tm // mix_rows, mix_tile, 0)

    @pl.when(s >= nv)
    def _():
        cols = pl.ds(pl.multiple_of((s - nv) * tn, tn), tn)
        u = jax.nn.gelu(_dot(xn_ref[...], _use_weight(win_ref, winb_ref)))
        h = (u * v_ref[:, cols]).astype(BF16)
        y_ref[...] += _dot(h, _use_weight(wout_ref, woutb_ref))


def _gmlp(x, g, w_in, ln_g, ln_b, ws_tiled, bs_tiled, w_out, *, tm, tn, chunk_len,
          v_period, layer=0, emit_bf16=False):
    rows, d = x.shape
    mix_rows = ws_tiled.shape[1]
    nv = D_GMLP // tn
    n_vblocks = rows // (tm * v_period)
    win_tile = lambda i, s: (0, jnp.where(s < nv, s + nv, s - nv))
    wout_tile = lambda i, s: (jnp.maximum(s - nv, 0), 0)
    out_specs = [pl.BlockSpec((tm, d), lambda i, s: (i, 0)),
                 pl.BlockSpec((mix_rows, D_GMLP), lambda i, s: (i // v_period, 0))]
    out_shape = [jax.ShapeDtypeStruct((rows, d), F32),
                 jax.ShapeDtypeStruct((n_vblocks * mix_rows, D_GMLP), F32)]
    if emit_bf16:
        assert rows == tm
        out_specs += [pl.BlockSpec((d, tn), win_tile), pl.BlockSpec((tn, d), wout_tile)]
        out_shape += [jax.ShapeDtypeStruct((d, 2 * D_GMLP), BF16),
                      jax.ShapeDtypeStruct((D_GMLP, d), BF16)]
    kern = functools.partial(_gmlp_kernel, nv=nv, tn=tn, chunk_len=chunk_len,
                             mix_rows=mix_rows, emit_bf16=emit_bf16)
    return pl.pallas_call(
        kern,
        grid=(rows // tm, 2 * nv),
        in_specs=[pl.BlockSpec((tm, d), lambda i, s: (i, 0)),
                  pl.BlockSpec((1, d), lambda i, s: (0, 0)),
                  _weight_spec(w_in, layer, (d, tn), win_tile),
                  pl.BlockSpec((1, D_GMLP), lambda i, s: (0, 0)),
                  pl.BlockSpec((1, D_GMLP), lambda i, s: (0, 0)),
                  pl.BlockSpec((N_SGU_GROUPS, mix_rows, mix_rows), lambda i, s: (0, 0, 0)),
                  pl.BlockSpec((mix_rows, N_SGU_GROUPS), lambda i, s: (0, 0)),
                  _weight_spec(w_out, layer, (tn, d), wout_tile)],
        out_specs=out_specs,
        out_shape=out_shape,
        scratch_shapes=[pltpu.VMEM((tm, d), BF16),
                        pltpu.VMEM((tm, D_GMLP), F32),
                        pltpu.VMEM((N_SGU_GROUPS, mix_rows, mix_rows), BF16)],
        compiler_params=_params(("arbitrary", "arbitrary")),
        name="gmlp",
    )(x, g.reshape(1, d), w_in, ln_g.reshape(1, D_GMLP), ln_b.reshape(1, D_GMLP),
      ws_tiled, bs_tiled, w_out)


def _top_blocks(gate, blk, n_take, axis):
    nb = gate.shape[axis]
    picks = []
    sel = jnp.zeros_like(gate)
    for r in range(MOBA_TOP_K):
        m = jnp.max(gate, axis=axis, keepdims=True)
        idx = jnp.min(jnp.where(gate == m, blk, float(nb)), axis=axis, keepdims=True)
        hit = blk == idx
        picks.append(idx)
        counts = 1.0 if n_take is None else jnp.where(r < n_take, 1.0, 0.0)
        sel = jnp.maximum(sel, jnp.where(hit, counts, 0.0))
        gate = jnp.where(hit, NEG_INF, gate)
    return picks, sel


def _kv_proj_kernel(x_ref, g_ref, wk_ref, wv_ref, k_ref, v_ref, kb_ref, vb_ref, kmean_ref, xn_ref):
    @pl.when(pl.program_id(1) == 0)
    def _():
        xn_ref[...] = _rmsnorm_rows(x_ref[...], g_ref[...]).astype(BF16)

    xn = xn_ref[...]
    k = _dot(xn, wk_ref[...])
    v = _dot(xn, wv_ref[...])
    k_ref[...] = k
    v_ref[...] = v
    kb_ref[...] = k.astype(BF16)
    vb_ref[...] = v.astype(BF16)
    for blk in range(kmean_ref.shape[0]):
        rows = slice(blk * MOBA_BLOCK, (blk + 1) * MOBA_BLOCK)
        kmean_ref[blk] = jnp.mean(k[rows, :], axis=0, keepdims=True)


def _kv_proj(x, g, wk, wv, *, tm, tn):
    rows, d = x.shape
    n = wk.shape[1]
    blocks_per_tile = tm // MOBA_BLOCK
    tile = lambda: pl.BlockSpec((tm, tn), lambda i, j: (i, j))
    return pl.pallas_call(
        _kv_proj_kernel,
        grid=(rows // tm, n // tn),
        in_specs=[pl.BlockSpec((tm, d), lambda i, j: (i, 0)),
                  pl.BlockSpec((1, d), lambda i, j: (0, 0)),
                  _weight_spec(wk, 0, (d, tn), lambda i, j: (0, j)),
                  _weight_spec(wv, 0, (d, tn), lambda i, j: (0, j))],
        out_specs=[tile(), tile(), tile(), tile(),
                   pl.BlockSpec((blocks_per_tile, 1, tn), lambda i, j: (i, 0, j))],
        out_shape=[jax.ShapeDtypeStruct((rows, n), F32), jax.ShapeDtypeStruct((rows, n), F32),
                   jax.ShapeDtypeStruct((rows, n), BF16), jax.ShapeDtypeStruct((rows, n), BF16),
                   jax.ShapeDtypeStruct((rows // MOBA_BLOCK, 1, n), F32)],
        scratch_shapes=[pltpu.VMEM((tm, d), BF16)],
        compiler_params=_params(("parallel", "arbitrary")),
        name="kv_proj",
    )(x, g.reshape(1, d), wk, wv)


HEADS_PER_STEP = 8
BLOCKS_PER_TRIP = 4


def _moba_prompt_kernel(slopes_ref, q_ref, kb_ref, vb_ref, kmean_ref, o_ref,
                        vt_ref, qaug_ref, kaug_ref, sel_ref, acc_ref):
    hg = pl.program_id(1)
    qt = pl.program_id(2)
    n_blocks = kb_ref.shape[1] // MOBA_BLOCK
    score_scale = (HEAD_DIM ** -0.5) * LOG2E
    heads = range(HEADS_PER_STEP)
    cols = [slice(hh * HEAD_DIM, (hh + 1) * HEAD_DIM) for hh in heads]
    slope2 = [slopes_ref[hg * HEADS_PER_STEP + hh] * LOG2E for hh in heads]

    @pl.when(qt == 0)
    def _():
        for j in range(n_blocks):
            rows = slice(j * MOBA_BLOCK, (j + 1) * MOBA_BLOCK)
            vt_ref[:, rows] = vb_ref[0, rows, :].astype(F32).T.astype(BF16)
        key_i = lax.broadcasted_iota(jnp.int32, (MOBA_BLOCK, HEAD_DIM), 0)
        piece = lax.broadcasted_iota(jnp.int32, (MOBA_BLOCK, HEAD_DIM), 1)
        kaug_ref[...] = jnp.where(piece < 3, key_i, 0).astype(BF16)
        part = lax.broadcasted_iota(jnp.int32, (HEAD_DIM, MOBA_BLOCK), 0)
        for hh in heads:
            whole = jnp.full((HEAD_DIM, MOBA_BLOCK), slope2[hh], F32)
            hi = whole.astype(BF16).astype(F32)
            mid = (whole - hi).astype(BF16).astype(F32)
            lo = whole - hi - mid
            pieces = jnp.where(part == 0, hi, jnp.where(part == 1, mid, jnp.where(part == 2, lo, 0.0)))
            qaug_ref[hh, HEAD_DIM:, :] = pieces.astype(BF16)

    key_i = lax.broadcasted_iota(jnp.int32, (MOBA_BLOCK, MOBA_BLOCK), 0)
    qry_i = lax.broadcasted_iota(jnp.int32, (MOBA_BLOCK, MOBA_BLOCK), 1)
    causal = qry_i >= key_i
    own = pl.ds(pl.multiple_of(qt * MOBA_BLOCK, MOBA_BLOCK), MOBA_BLOCK)
    q_ts = [q_ref[0, :, cols[hh]].T for hh in heads]
    for hh in heads:
        qaug_ref[hh, :HEAD_DIM, :] = (q_ts[hh] * score_scale).astype(BF16)
    kaug = kaug_ref[...]

    def scores(rows, hh):
        keys = jnp.concatenate([kb_ref[0, rows, cols[hh]], kaug], axis=1)
        return _dot(keys, qaug_ref[hh])

    raw = [scores(own, hh) for hh in heads]
    gates = [jnp.dot(kmean_ref[0, :, cols[hh]], q_ts[hh], precision=lax.Precision.HIGHEST,
                     preferred_element_type=F32) for hh in heads]
    stats = []
    for hh in heads:
        blk = lax.broadcasted_iota(jnp.int32, gates[hh].shape, 0)
        gate = jnp.where(blk < qt, gates[hh], NEG_INF)
        _, sel = _top_blocks(gate, blk.astype(F32), qt, axis=0)
        sel_ref[hh] = sel
        t = jnp.where(causal, raw[hh], NEG_INF)
        m0 = jnp.max(t, axis=0, keepdims=True)
        p = jnp.exp2(t - m0)
        l0 = jnp.sum(p, axis=0, keepdims=True)
        acc_ref[hh] = _dot(vt_ref[cols[hh], own], p.astype(BF16))
        stats.append((m0, l0))

    def past_blocks(j0, n_now, stats):
        blocks = range(n_now)
        rows = [pl.ds(pl.multiple_of((j0 + i) * MOBA_BLOCK, MOBA_BLOCK), MOBA_BLOCK) for i in blocks]
        raw = [[scores(rows[i], hh) for hh in heads] for i in blocks]
        stats = list(stats)
        for i in blocks:
            blocks_between = ((qt - j0 - i) * MOBA_BLOCK).astype(F32)
            for hh in heads:
                m, l = stats[hh]
                t = raw[i][hh]
                far = slope2[hh] * blocks_between
                picked = sel_ref[hh, pl.ds(j0 + i, 1), :] > 0.0
                m_blk = jnp.max(t, axis=0, keepdims=True) - far
                m_new = jnp.where(picked, jnp.maximum(m, m_blk), m)
                shift = jnp.where(picked, m_new + far, jnp.inf)
                p = jnp.exp2(t - shift)
                alpha = jnp.exp2(m - m_new)
                l = alpha * l + jnp.sum(p, axis=0, keepdims=True)
                acc_ref[hh] = alpha * acc_ref[hh] + _dot(vt_ref[cols[hh], rows[i]], p.astype(BF16))
                stats[hh] = (m_new, l)
        return tuple(stats)

    stats = lax.fori_loop(0, qt // BLOCKS_PER_TRIP,
                          lambda i, st: past_blocks(i * BLOCKS_PER_TRIP, BLOCKS_PER_TRIP, st),
                          tuple(stats))
    done = (qt // BLOCKS_PER_TRIP) * BLOCKS_PER_TRIP
    stats = lax.fori_loop(done, qt, lambda j, st: past_blocks(j, 1, st), stats)
    for hh in heads:
        o_ref[0, :, cols[hh]] = (acc_ref[hh] / stats[hh][1]).T.astype(o_ref.dtype)


def _moba_prompt(q, kb, vb, kmean, slopes):
    bsz, seq, _ = q.shape
    n_qt = seq // MOBA_BLOCK
    width = HEADS_PER_STEP * HEAD_DIM
    return pl.pallas_call(
        _moba_prompt_kernel,
        grid_spec=pltpu.PrefetchScalarGridSpec(
            num_scalar_prefetch=0,
            grid=(bsz, N_HEADS // HEADS_PER_STEP, n_qt),
            in_specs=[pl.BlockSpec(memory_space=pltpu.SMEM),
                      pl.BlockSpec((1, MOBA_BLOCK, width), lambda b, h, t: (b, t, h)),
                      pl.BlockSpec((1, seq, width), lambda b, h, t: (b, 0, h),
                                   pipeline_mode=pl.Buffered(1)),
                      pl.BlockSpec((1, seq, width), lambda b, h, t: (b, 0, h),
                                   pipeline_mode=pl.Buffered(1)),
                      pl.BlockSpec((1, n_qt, width), lambda b, h, t: (b, 0, h))],
            out_specs=pl.BlockSpec((1, MOBA_BLOCK, width), lambda b, h, t: (b, t, h)),
            scratch_shapes=[pltpu.VMEM((width, seq), BF16),
                            pltpu.VMEM((HEADS_PER_STEP, 2 * HEAD_DIM, MOBA_BLOCK), BF16),
                            pltpu.VMEM((MOBA_BLOCK, HEAD_DIM), BF16),
                            pltpu.VMEM((HEADS_PER_STEP, n_qt, MOBA_BLOCK), F32),
                            pltpu.VMEM((HEADS_PER_STEP, HEAD_DIM, MOBA_BLOCK), F32)]),
        out_shape=jax.ShapeDtypeStruct(q.shape, BF16),
        compiler_params=_params(("parallel", "parallel", "arbitrary")),
        name="moba_prompt",
    )(slopes, q, kb, vb, kmean)


def _page_block_means(page_refs, o_ref, ppb):
    for blk in range(len(page_refs) // ppb):
        total = jnp.sum(page_refs[blk * ppb][0], axis=0)
        for p in range(1, ppb):
            total = total + jnp.sum(page_refs[blk * ppb + p][0], axis=0)
        o_ref[0, blk] = total / MOBA_BLOCK


def _ffn_hosting_means_kernel(pt_ref, x_ref, g_ref, w1_ref, w2_ref, *rest, n_page_refs, ppb):
    del pt_ref
    page_refs = rest[:n_page_refs]
    o_ref, kmean_ref, xn_ref = rest[n_page_refs:]
    _ffn_step(x_ref, g_ref, w1_ref, w2_ref, o_ref, xn_ref)
    _page_block_means(page_refs, kmean_ref, ppb)


def _ffn_hosting_means(x, g, w1, w2, cache_k, pt_flat, db, n_pages, *, tm, tf):
    rows, d = x.shape
    dff = w1.shape[-1]
    _, page, n_kv, dh = cache_k.shape
    ppb = MOBA_BLOCK // page
    nfb = (n_pages * page) // MOBA_BLOCK
    n_j = dff // tf
    n_steps = (rows // tm) * n_j
    assert (db * nfb) % n_steps == 0 and nfb % ((db * nfb) // n_steps) == 0
    blocks_per_step = (db * nfb) // n_steps
    pages_per_step = blocks_per_step * ppb
    steps_per_batch = nfb // blocks_per_step

    def page_spec(p):
        def index_map(i, j, pt):
            step = i * n_j + j
            logical = (step % steps_per_batch) * pages_per_step + p
            return (pt[(step // steps_per_batch) * n_pages + logical], 0, 0, 0)
        return pl.BlockSpec((1, page, n_kv, dh), index_map)

    def means_map(i, j, pt):
        step = i * n_j + j
        return (step // steps_per_batch, step % steps_per_batch, 0, 0)

    kern = functools.partial(_ffn_hosting_means_kernel, n_page_refs=pages_per_step, ppb=ppb)
    return pl.pallas_call(
        kern,
        grid_spec=pltpu.PrefetchScalarGridSpec(
            num_scalar_prefetch=1,
            grid=(rows // tm, n_j),
            in_specs=[pl.BlockSpec((tm, d), lambda i, j, pt: (i, 0), pipeline_mode=pl.Buffered(1)),
                      pl.BlockSpec((1, d), lambda i, j, pt: (0, 0)),
                      pl.BlockSpec((d, tf), lambda i, j, pt: (0, j)),
                      pl.BlockSpec((tf, d), lambda i, j, pt: (j, 0))]
                     + [page_spec(p) for p in range(pages_per_step)],
            out_specs=[pl.BlockSpec((tm, d), lambda i, j, pt: (i, 0)),
                       pl.BlockSpec((1, blocks_per_step, n_kv, dh), means_map)],
            scratch_shapes=[pltpu.VMEM((tm, d), BF16)]),
        out_shape=[jax.ShapeDtypeStruct((rows, d), F32),
                   jax.ShapeDtypeStruct((db, nfb, n_kv, dh), F32)],
        compiler_params=_params(("arbitrary", "arbitrary")),
        name="ffn_hosting_means",
    )(pt_flat, x, g.reshape(1, d), w1, w2, *([cache_k] * pages_per_step))


def _sample_topk_kernel(q_ref, km_ref, o_ref):
    t = q_ref.shape[1]
    lane = lax.broadcasted_iota(jnp.int32, (t, 128), 1)
    for h in range(N_HEADS):
        cols = slice(h * HEAD_DIM, (h + 1) * HEAD_DIM)
        gate = _dot_nt(q_ref[0, :, cols], km_ref[0, h], precision=lax.Precision.HIGHEST)
        blk_f = lax.broadcasted_iota(jnp.int32, gate.shape, 1).astype(F32)
        picks, _ = _top_blocks(gate, blk_f, None, axis=1)
        out = jnp.zeros((t, 128), F32)
        for r, idx in enumerate(picks):
            out = jnp.where(lane == r, idx, out)
        o_ref[0, h] = out.astype(jnp.int32)


def _sample_topk(q, kmean):
    db, t, _ = q.shape
    nfb = kmean.shape[2]
    assert nfb >= MOBA_TOP_K
    out = pl.pallas_call(
        _sample_topk_kernel,
        grid=(db,),
        in_specs=[pl.BlockSpec((1, t, D_MODEL), lambda b: (b, 0, 0)),
                  pl.BlockSpec((1, N_HEADS, nfb, HEAD_DIM), lambda b: (b, 0, 0, 0))],
        out_specs=pl.BlockSpec((1, N_HEADS, t, 128), lambda b: (b, 0, 0, 0)),
        out_shape=jax.ShapeDtypeStruct((db, N_HEADS, t, 128), jnp.int32),
        compiler_params=_params(("parallel",)),
        name="sample_topk",
    )(q, kmean)
    return out[..., :MOBA_TOP_K]


def _sample_pages_fetch(step, n_steps, idx_ref, pt_ref, ck_hbm, cv_hbm, kbuf, vbuf, sems,
                        *, n_t, n_sel_pages, ppb, n_pages):
    slot = step % 2

    def page_copies(at_step, sl):
        bb = at_step // N_HEADS
        hh = at_step % N_HEADS
        copies = []
        for t in range(n_t):
            for n in range(n_sel_pages):
                block = idx_ref[(at_step * n_t + t) * MOBA_TOP_K + n // ppb]
                phys = pt_ref[bb * n_pages + block * ppb + n % ppb]
                dst = t * n_sel_pages + n
                copies.append(pltpu.make_async_copy(ck_hbm.at[phys, :, hh, :], kbuf.at[sl, dst],
                                                    sems.at[0, sl]))
                copies.append(pltpu.make_async_copy(cv_hbm.at[phys, :, hh, :], vbuf.at[sl, dst],
                                                    sems.at[1, sl]))
        return copies

    @pl.when(step == 0)
    def _():
        for cp in page_copies(step, slot):
            cp.start()

    @pl.when(step + 1 < n_steps)
    def _():
        for cp in page_copies(step + 1, 1 - slot):
            cp.start()

    for cp in page_copies(step, slot):
        cp.wait()


def _sample_attention_step(step, idx_ref, slope, q_ref, kn_ref, vn_ref, o_ref, kbuf, vbuf,
                           *, n_sel_pages, ppb, n_pages):
    n_t = q_ref.shape[1]
    page = kbuf.shape[2]
    past_len = n_pages * page
    slot = step % 2
    scale = HEAD_DIM ** -0.5
    off_bias = slope * lax.broadcasted_iota(jnp.int32, (page, 1), 0).astype(F32)
    new_i = lax.broadcasted_iota(jnp.int32, (n_t, 1), 0)
    for t in range(n_t):
        q = q_ref[0, t:t + 1, :]
        pos_q = past_len + t
        scores = []
        for n in range(n_sel_pages):
            s = jnp.sum(kbuf[slot, t * n_sel_pages + n] * q, axis=1, keepdims=True) * scale
            block = idx_ref[(step * n_t + t) * MOBA_TOP_K + n // ppb]
            to_page = (pos_q - (block * ppb + n % ppb) * page).astype(F32)
            scores.append(s + off_bias - slope * to_page)
        s_new = jnp.sum(kn_ref[0] * q, axis=1, keepdims=True) * scale
        d_new = t - new_i
        s_new = jnp.where(d_new >= 0, s_new - slope * d_new.astype(F32), NEG_INF)

        m = jnp.max(s_new, axis=0, keepdims=True)
        for s in scores:
            m = jnp.maximum(m, jnp.max(s, axis=0, keepdims=True))
        p_new = jnp.exp(s_new - m)
        l = jnp.sum(p_new, axis=0, keepdims=True)
        acc = jnp.sum(p_new * vn_ref[0], axis=0, keepdims=True)
        for n, s in enumerate(scores):
            p = jnp.exp(s - m)
            l = l + jnp.sum(p, axis=0, keepdims=True)
            acc = acc + jnp.sum(p * vbuf[slot, t * n_sel_pages + n], axis=0, keepdims=True)
        o_ref[0, t:t + 1, :] = acc / l


def _ffn_hosting_attention_kernel(idx_ref, pt_ref, slopes_ref, x_ref, g_ref, w1_ref, w2_ref,
                                  q_ref, kn_ref, vn_ref, ck_hbm, cv_hbm, o_ref, att_ref,
                                  xn_ref, kbuf, vbuf, sems, *, n_sel_pages, ppb, n_pages):
    i = pl.program_id(0)
    j = pl.program_id(1)
    step = i * pl.num_programs(1) + j
    n_steps = pl.num_programs(0) * pl.num_programs(1)
    _sample_pages_fetch(step, n_steps, idx_ref, pt_ref, ck_hbm, cv_hbm, kbuf, vbuf, sems,
                        n_t=q_ref.shape[1], n_sel_pages=n_sel_pages, ppb=ppb, n_pages=n_pages)

    _ffn_step(x_ref, g_ref, w1_ref, w2_ref, o_ref, xn_ref)
    _sample_attention_step(step, idx_ref, slopes_ref[step % N_HEADS], q_ref, kn_ref, vn_ref,
                           att_ref, kbuf, vbuf, n_sel_pages=n_sel_pages, ppb=ppb, n_pages=n_pages)


def _ffn_hosting_attention(x, g, w1, w2, q, k_new, v_new, cache_k, cache_v, idx_flat, pt_flat,
                           slopes, n_pages, *, tm, tf):
    rows, d = x.shape
    dff = w1.shape[-1]
    db, t, _ = q.shape
    page = cache_k.shape[1]
    ppb = MOBA_BLOCK // page
    n_sel_pages = MOBA_TOP_K * ppb
    n_j = dff // tf
    assert (rows // tm) * n_j == db * N_HEADS

    def new_spec():
        return pl.BlockSpec((1, t, HEAD_DIM),
                            lambda i, j, idx, pt: ((i * n_j + j) // N_HEADS, 0, (i * n_j + j) % N_HEADS))

    kern = functools.partial(_ffn_hosting_attention_kernel, n_sel_pages=n_sel_pages, ppb=ppb,
                             n_pages=n_pages)
    return pl.pallas_call(
        kern,
        grid_spec=pltpu.PrefetchScalarGridSpec(
            num_scalar_prefetch=2,
            grid=(rows // tm, n_j),
            in_specs=[pl.BlockSpec(memory_space=pltpu.SMEM),
                      pl.BlockSpec((tm, d), lambda i, j, idx, pt: (i, 0)),
                      pl.BlockSpec((1, d), lambda i, j, idx, pt: (0, 0)),
                      pl.BlockSpec((d, tf), lambda i, j, idx, pt: (0, j)),
                      pl.BlockSpec((tf, d), lambda i, j, idx, pt: (j, 0)),
                      new_spec(), new_spec(), new_spec(),
                      pl.BlockSpec(memory_space=pl.ANY), pl.BlockSpec(memory_space=pl.ANY)],
            out_specs=[pl.BlockSpec((tm, d), lambda i, j, idx, pt: (i, 0)), new_spec()],
            scratch_shapes=[pltpu.VMEM((tm, d), BF16),
                            pltpu.VMEM((2, t * n_sel_pages, page, HEAD_DIM), F32),
                            pltpu.VMEM((2, t * n_sel_pages, page, HEAD_DIM), F32),
                            pltpu.SemaphoreType.DMA((2, 2))]),
        out_shape=[jax.ShapeDtypeStruct((rows, d), F32), jax.ShapeDtypeStruct(q.shape, F32)],
        compiler_params=_params(("arbitrary", "arbitrary")),
        name="ffn_hosting_attention",
    )(idx_flat, pt_flat, slopes, x, g.reshape(1, d), w1, w2, q, k_new, v_new, cache_k, cache_v)


def kernel(x_prompt, x_sample, cache_k, cache_v, page_table, norm_mix_g, norm_ffn_g,
           gmlp_w_in, gmlp_ln_g, gmlp_ln_b, gmlp_w_s, gmlp_b_s, gmlp_w_out,
           kv_norm_g, w_k, w_v, attn_w_q, attn_w_o, ffn_w1, ffn_w2, final_norm_g):
    bp, sp, d = x_prompt.shape
    bs, ss, _ = x_sample.shape
    depth = norm_mix_g.shape[0]
    n_a = gmlp_w_in.shape[0]
    _, page, n_kv, dh = cache_k.shape
    n_pages = page_table.shape[1]
    assert sp % MOBA_BLOCK == 0 and sp % CHUNK == 0 and ss <= CHUNK
    assert (n_pages * page) % MOBA_BLOCK == 0
    assert n_kv == N_HEADS and dh == HEAD_DIM

    slopes = jnp.exp2(-8.0 * jnp.arange(1, N_HEADS + 1, dtype=F32) / N_HEADS)

    rows_p, rows_s = bp * sp, bs * ss
    xp = x_prompt.reshape(rows_p, d)
    xs = x_sample.reshape(rows_s, d)
    pt_flat = page_table.reshape(-1)

    tm_p = 512
    assert n_a >= 2
    sgu_p, sgu_s = [], []
    att_s = None
    for layer in range(depth):
        g = norm_mix_g[layer]
        last = layer == depth - 1
        if layer < n_a:
            ws_s = jnp.tile(gmlp_w_s[layer][:, :ss, :ss], (1, bs, bs))
            bs_s = jnp.tile(gmlp_b_s[layer][:, :ss].T, (bs, 1))
            xs, vs, w_in, w_out = _gmlp(
                xs, g, gmlp_w_in, gmlp_ln_g[layer], gmlp_ln_b[layer], ws_s, bs_s, gmlp_w_out,
                tm=rows_s, tn=512, chunk_len=ss, v_period=1, layer=layer, emit_bf16=True)
            sgu_s.append(vs.reshape(bs, ss, D_GMLP))
        else:
            a = layer - n_a
            xs, wo_b = _proj_residual(att_s.reshape(rows_s, d), attn_w_o, xs, tm=rows_s, tn=512,
                                      layer=a, emit_bf16=True)
        xs, w1_b, w2_b = _ffn(xs, norm_ffn_g[layer], ffn_w1, ffn_w2, final_norm_g, tm=rows_s,
                              tf=512, final_norm=last, layer=layer, emit_bf16=True)

        ahead = None
        if layer + 1 >= n_a and not last:
            if layer + 1 == n_a:
                ks_, vs_, wk_b, wv_b = _norm_matmul(xs, kv_norm_g, [w_k, w_v], tm=rows_s, tn=512,
                                                   emit_bf16=True)
                kmean_s = kmean_s.transpose(0, 2, 1, 3)
            qs, wq_next = _norm_matmul(xs, norm_mix_g[layer + 1], [attn_w_q], tm=rows_s, tn=512,
                                       layer=layer + 1 - n_a, emit_bf16=True)
            qs3 = qs.reshape(bs, ss, d)
            ahead = (qs3, _sample_topk(qs3, kmean_s).reshape(-1))

        if layer < n_a:
            xp, vp = _gmlp(xp, g, w_in, gmlp_ln_g[layer], gmlp_ln_b[layer], gmlp_w_s[layer],
                           gmlp_b_s[layer].T, w_out, tm=tm_p, tn=1024, chunk_len=CHUNK,
                           v_period=sp // tm_p)
            sgu_p.append(vp.reshape(bp, CHUNK, D_GMLP))
        else:
            if layer == n_a:
                kp, vp_, kb_p, vb_p, kmean_p = _kv_proj(xp, kv_norm_g, wk_b, wv_b, tm=256, tn=d)
            (qp,) = _norm_matmul(xp, g, [wq_b], tm=tm_p, tn=d)
            att_p = _moba_prompt(qp.reshape(bp, sp, d), kb_p.reshape(bp, sp, d),
                                 vb_p.reshape(bp, sp, d),
                                 kmean_p.reshape(bp, sp // MOBA_BLOCK, d), slopes)
            (xp,) = _proj_residual(att_p.reshape(rows_p, d), wo_b, xp, tm=tm_p, tn=d)
        if layer == n_a - 2:
            xp, kmean_s = _ffn_hosting_means(xp, norm_ffn_g[layer], w1_b, w2_b, cache_k, pt_flat,
                                             bs, n_pages, tm=tm_p, tf=1024)
        elif ahead is None:
            (xp,) = _ffn(xp, norm_ffn_g[layer], w1_b, w2_b, final_norm_g, tm=tm_p, tf=1024,
                         final_norm=last)
        else:
            xp, att_s = _ffn_hosting_attention(
                xp, norm_ffn_g[layer], w1_b, w2_b, ahead[0], ks_.reshape(bs, ss, d),
                vs_.reshape(bs, ss, d), cache_k, cache_v, ahead[1], pt_flat, slopes, n_pages,
                tm=tm_p, tf=1024)
            wq_b = wq_next

    kv_shape_p = (bp, sp, n_kv, dh)
    kv_shape_s = (bs, ss, n_kv, dh)
    return (xp.reshape(bp, sp, d), xs.reshape(bs, ss, d),
            kp.reshape(kv_shape_p), vp_.reshape(kv_shape_p),
            ks_.reshape(kv_shape_s), vs_.reshape(kv_shape_s),
            jnp.stack(sgu_p), jnp.stack(sgu_s))
```
